```python
import math
import jax, jax.numpy as jnp
from jax import lax
import numpy as np

D_MODEL = 2048
BATCH = 1
SEQ = 8192
DEPTH = 2
DEC_BATCH = 128
DEC_SEQ = 8
PAST_LEN = 2048
PAGE_SIZE = 128

NSA_HEADS = 8
NSA_KV_HEADS = 2
NSA_GROUP = NSA_HEADS // NSA_KV_HEADS
HEAD_DIM = 128
NSA_WIDTH = NSA_HEADS * HEAD_DIM
CMP_STRIDE = 16
CMP_BLOCK = 2 * CMP_STRIDE
SEL_BLOCK = 64
SEG_PER_SEL = SEL_BLOCK // CMP_STRIDE
N_SELECT = 16
N_LOCAL_SEL = 2
WINDOW = 512
WIN_BLOCK = 128
Q_BLOCK = 128
SCALE = HEAD_DIM ** -0.5
RET_HEADS = 4
RET_DK = 256
RET_DV = 256
RET_QK = RET_HEADS * RET_DK
RET_WIDTH = RET_HEADS * RET_DV
RET_CHUNK = 128
N_GROUPS = 4
EXPERTS_PER_GROUP = 8
N_EXPERTS = N_GROUPS * EXPERTS_PER_GROUP
TOP_K_IN_GROUP = 2
D_EXPERT = 1024
N_MOD = 6
EPS = 1e-6
NEG_INF = -1e30
KV_COLS = 3 * 2 * NSA_KV_HEADS * HEAD_DIM
OFF_NSA_KV = NSA_WIDTH
OFF_NSA_G = OFF_NSA_KV + KV_COLS
OFF_RET = OFF_NSA_G + 3 * NSA_HEADS
OFF_MERGE = OFF_RET + 2 * RET_QK + 2 * RET_WIDTH
IN_COLS = OFF_MERGE + 2 * D_MODEL

kernel_name = 'hybrid_nsa_retention_hmoe_step'


def rmsnorm(x, g):
    xf = x.astype(jnp.float32)
    y = xf * lax.rsqrt(jnp.mean(xf * xf, axis=-1, keepdims=True) + EPS)
    return (y * g.astype(jnp.float32)).astype(x.dtype)


def masked_softmax(logits, mask, axis=-1):
    p = jax.nn.softmax(jnp.where(mask, logits, NEG_INF), axis=axis)
    return jnp.where(mask, p, 0.0)


def alibi_slopes():
    h = jnp.arange(1, NSA_HEADS + 1, dtype=jnp.float32)
    return jnp.exp2(-8.0 * h / NSA_HEADS).reshape(NSA_KV_HEADS, NSA_GROUP)


def adaln(c, w_ada, b_ada):
    m = (jax.nn.silu(c) @ w_ada + b_ada).reshape(c.shape[0], N_MOD, D_MODEL)
    return [m[:, i, None, :] for i in range(N_MOD)]


def split_proj(z):
    B, T = z.shape[:2]
    q = z[..., :OFF_NSA_KV].reshape(B, T, NSA_KV_HEADS, NSA_GROUP, HEAD_DIM)
    kv = z[..., OFF_NSA_KV:OFF_NSA_G].reshape(B, T, 3, 2, NSA_KV_HEADS, HEAD_DIM)
    g_nsa = jax.nn.sigmoid(z[..., OFF_NSA_G:OFF_RET]).reshape(B, T, 3, NSA_KV_HEADS, NSA_GROUP)
    r = z[..., OFF_RET:OFF_MERGE]
    rq = r[..., :RET_QK].reshape(B, T, RET_HEADS, RET_DK)
    rk = r[..., RET_QK:2 * RET_QK].reshape(B, T, RET_HEADS, RET_DK)
    rv = r[..., 2 * RET_QK:2 * RET_QK + RET_WIDTH].reshape(B, T, RET_HEADS, RET_DV)
    rg = r[..., 2 * RET_QK + RET_WIDTH:]
    gm = jax.nn.sigmoid(z[..., OFF_MERGE:]).reshape(B, T, 2, D_MODEL)
    return q, kv, g_nsa, rq, rk, rv, rg, gm


def compress_rows(rows, pe, phi):
    B, L = rows.shape[:2]
    n_seg = L // CMP_STRIDE
    seg = rows.reshape(B, n_seg, CMP_STRIDE, NSA_KV_HEADS, HEAD_DIM)
    nxt = jnp.concatenate([seg[:, 1:], jnp.zeros_like(seg[:, :1])], axis=1)
    blk = jnp.concatenate([seg, nxt], axis=2) + pe[None, None, :, None, :]
    return jnp.einsum('bnlgd,lde->bnge', blk, phi.reshape(CMP_BLOCK, HEAD_DIM, HEAD_DIM))


def compressed_attn(q, kc, vc, q_pos, slopes):
    n_blk = kc.shape[1]
    end = jnp.arange(n_blk) * CMP_STRIDE + (CMP_BLOCK - 1)
    dist = q_pos[:, None] - end[None, :]
    s = (jnp.einsum('bqgrd,bngd->bgrqn', q, kc).astype(jnp.float32) * SCALE
         - slopes[None, :, :, None, None] * dist.astype(jnp.float32))
    p = masked_softmax(s, dist >= 0)
    o = jnp.einsum('bgrqn,bngd->bqgrd', p.astype(vc.dtype), vc)
    return o, p


def select_blocks(p_cmp, q_pos):
    imp = p_cmp.sum(axis=2)
    seg = imp + jnp.pad(imp[..., :-1], ((0, 0), (0, 0), (0, 0), (1, 0)))
    n_sel = imp.shape[-1] // SEG_PER_SEL
    sel = seg.reshape(seg.shape[:-1] + (n_sel, SEG_PER_SEL)).sum(-1)
    j = jnp.arange(n_sel)[None, :]
    cur = (q_pos // SEL_BLOCK)[:, None]
    valid = j <= cur
    forced = (j == 0) | (valid & (j > cur - N_LOCAL_SEL))
    score = jnp.where(forced, 1e9, jnp.where(valid, sel, -1e9))
    _, idx = lax.top_k(score, min(N_SELECT, n_sel))
    return idx


def selected_attn(q, kb, vb, idx, q_pos, slopes):
    B = q.shape[0]
    bi = jnp.arange(B)[:, None, None, None]
    gi = jnp.arange(NSA_KV_HEADS)[None, :, None, None]
    kg = kb[bi, gi, idx]
    vg = vb[bi, gi, idx]
    pos = idx[..., None] * SEL_BLOCK + jnp.arange(SEL_BLOCK)
    dist = q_pos[None, None, :, None, None] - pos
    s = (jnp.einsum('bqgrd,bgqksd->bgrqks', q, kg).astype(jnp.float32) * SCALE
         - slopes[None, :, :, None, None, None] * dist[:, :, None].astype(jnp.float32))
    p = masked_softmax(s, (dist >= 0)[:, :, None], axis=(-2, -1))
    return jnp.einsum('bgrqks,bgqksd->bqgrd', p.astype(vg.dtype), vg)


def window_attn(q, k, v, dist, slopes):
    mask = (dist >= 0) & (dist < WINDOW)
    s = (jnp.einsum('bqgrd,bkgd->bgrqk', q, k).astype(jnp.float32) * SCALE
         - slopes[None, :, :, None, None] * dist.astype(jnp.float32))
    p = masked_softmax(s, mask)
    return jnp.einsum('bgrqk,bkgd->bqgrd', p.astype(v.dtype), v)


def combine_nsa(g, o_cmp, o_slc, o_win):
    B, T = g.shape[:2]
    o = (g[:, :, 0, :, :, None] * o_cmp + g[:, :, 1, :, :, None] * o_slc
         + g[:, :, 2, :, :, None] * o_win)
    return o.reshape(B, T, NSA_WIDTH)


def nsa_prompt(q, kv, g, pe, phi, slopes):
    B, T = q.shape[:2]
    pos = jnp.arange(T)
    kc = compress_rows(kv[:, :, 0, 0], pe[0], phi[0])
    vc = compress_rows(kv[:, :, 0, 1], pe[1], phi[1])
    o_cmp, p_cmp = compressed_attn(q, kc, vc, pos, slopes)
    idx = select_blocks(p_cmp, pos)
    n_sel = T // SEL_BLOCK
    kb = kv[:, :, 1, 0].reshape(B, n_sel, SEL_BLOCK, NSA_KV_HEADS, HEAD_DIM).transpose(0, 3, 1, 2, 4)
    vb = kv[:, :, 1, 1].reshape(B, n_sel, SEL_BLOCK, NSA_KV_HEADS, HEAD_DIM).transpose(0, 3, 1, 2, 4)
    nqb = T // Q_BLOCK
    q_b = q.reshape(B, nqb, Q_BLOCK, NSA_KV_HEADS, NSA_GROUP, HEAD_DIM).swapaxes(0, 1)
    pos_b = pos.reshape(nqb, Q_BLOCK)
    idx_b = idx.reshape(B, NSA_KV_HEADS, nqb, Q_BLOCK, -1).transpose(2, 0, 1, 3, 4)
    o_slc = lax.map(lambda a: selected_attn(a[0], kb, vb, a[2], a[1], slopes), (q_b, pos_b, idx_b))
    o_slc = o_slc.swapaxes(0, 1).reshape(B, T, NSA_KV_HEADS, NSA_GROUP, HEAD_DIM)
    nb = T // WIN_BLOCK
    n_prev = WINDOW // WIN_BLOCK
    band = (n_prev + 1) * WIN_BLOCK

    def bands(rows):
        padded = jnp.concatenate([jnp.zeros((B, n_prev * WIN_BLOCK) + rows.shape[2:], rows.dtype), rows], axis=1)
        padded = padded.reshape(B, nb + n_prev, WIN_BLOCK, NSA_KV_HEADS, HEAD_DIM)
        stacked = jnp.stack([padded[:, j:j + nb] for j in range(n_prev + 1)], axis=2)
        return stacked.reshape(B, nb, band, NSA_KV_HEADS, HEAD_DIM)

    blk0 = jnp.arange(nb)[:, None] * WIN_BLOCK
    q_pos = blk0 + jnp.arange(WIN_BLOCK)[None, :]
    k_pos = blk0 - n_prev * WIN_BLOCK + jnp.arange(band)[None, :]
    dist = jnp.where(k_pos[:, None, :] >= 0, q_pos[:, :, None] - k_pos[:, None, :], -1)
    q_w = q.reshape(B, nb, WIN_BLOCK, NSA_KV_HEADS, NSA_GROUP, HEAD_DIM)
    o_win = jax.vmap(window_attn, in_axes=(1, 1, 1, 0, None), out_axes=1)(
        q_w, bands(kv[:, :, 2, 0]), bands(kv[:, :, 2, 1]), dist, slopes)
    o_win = o_win.reshape(B, T, NSA_KV_HEADS, NSA_GROUP, HEAD_DIM)
    return combine_nsa(g, o_cmp, o_slc, o_win)


def nsa_sample(q, kv, g, cmp_pages, slc_pages, win_buf, page_table, pe, phi, slopes):
    Bd, Tq = q.shape[:2]
    past = page_table.shape[1] * cmp_pages.shape[1]
    total = past + Tq
    l_pad = -(-total // SEL_BLOCK) * SEL_BLOCK
    q_pos = past + jnp.arange(Tq)

    def full_rows(pages, new):
        old = pages[page_table].reshape((Bd, past) + pages.shape[2:])
        rows = jnp.concatenate([old, new.astype(old.dtype)], axis=1)
        return jnp.pad(rows, ((0, 0), (0, l_pad - total), (0, 0), (0, 0), (0, 0)))

    cmp_rows = full_rows(cmp_pages, kv[:, :, 0])
    slc_rows = full_rows(slc_pages, kv[:, :, 1])
    kc = compress_rows(cmp_rows[:, :, 0], pe[0], phi[0])
    vc = compress_rows(cmp_rows[:, :, 1], pe[1], phi[1])
    o_cmp, p_cmp = compressed_attn(q, kc, vc, q_pos, slopes)
    idx = select_blocks(p_cmp, q_pos)
    n_sel = l_pad // SEL_BLOCK
    kb = slc_rows[:, :, 0].reshape(Bd, n_sel, SEL_BLOCK, NSA_KV_HEADS, HEAD_DIM).transpose(0, 3, 1, 2, 4)
    vb = slc_rows[:, :, 1].reshape(Bd, n_sel, SEL_BLOCK, NSA_KV_HEADS, HEAD_DIM).transpose(0, 3, 1, 2, 4)
    o_slc = selected_attn(q, kb, vb, idx, q_pos, slopes)
    wb = win_buf.shape[1]
    win_rows = jnp.concatenate([win_buf, kv[:, :, 2].astype(win_buf.dtype)], axis=1)
    k_pos = past - wb + jnp.arange(wb + Tq)
    dist = q_pos[:, None] - k_pos[None, :]
    o_win = window_attn(q, win_rows[:, :, 0], win_rows[:, :, 1], dist, slopes)
    return combine_nsa(g, o_cmp, o_slc, o_win), win_rows[:, Tq:]


def retention_chunked(q, k, v, s0, chunk):
    B, T = q.shape[:2]
    nc = T // chunk
    log_gamma = jnp.log1p(-jnp.exp2(-5.0 - jnp.arange(RET_HEADS, dtype=jnp.float32)))
    i = jnp.arange(chunk, dtype=jnp.float32)
    diff = i[:, None] - i[None, :]
    decay_in = jnp.where(diff >= 0, jnp.exp(jnp.maximum(diff, 0.0)[None] * log_gamma[:, None, None]), 0.0)
    decay_q = jnp.exp((i + 1.0)[None, :] * log_gamma[:, None])
    decay_k = jnp.exp((chunk - 1.0 - i)[None, :] * log_gamma[:, None])
    decay_c = jnp.exp(chunk * log_gamma)

    def to_chunks(a):
        return a.astype(jnp.float32).reshape(B, nc, chunk, RET_HEADS, -1).transpose(1, 0, 3, 2, 4)

    def step(s, inp):
        qc, kc, vc = inp
        att = jnp.einsum('bhqd,bhkd->bhqk', qc, kc) * decay_in
        o = (jnp.einsum('bhqk,bhkv->bhqv', att, vc)
             + jnp.einsum('bhqd,bhdv->bhqv', qc, s) * decay_q[None, :, :, None])
        s = s * decay_c[None, :, None, None] + jnp.einsum('bhkd,bhkv->bhdv', kc * decay_k[None, :, :, None], vc)
        return s, o

    s, o = lax.scan(step, s0.astype(jnp.float32), (to_chunks(q), to_chunks(k), to_chunks(v)))
    return o.transpose(1, 0, 3, 2, 4).reshape(B, T, RET_HEADS, RET_DV), s


def retention_branch(rq, rk, rv, rg, s0, gn_g):
    B, T = rq.shape[:2]
    o, s = retention_chunked(rq, rk * RET_DK ** -0.5, rv, s0, math.gcd(RET_CHUNK, T))
    mu = o.mean(-1, keepdims=True)
    var = jnp.mean(jnp.square(o - mu), -1, keepdims=True)
    y = ((o - mu) * lax.rsqrt(var + EPS)).reshape(B, T, RET_WIDTH) * gn_g.astype(jnp.float32)
    return (jax.nn.silu(rg.astype(jnp.float32)) * y).astype(rg.dtype), s


def merge_branches(o_nsa, y_ret, gm, w_a, w_b, w_o):
    return (gm[:, :, 0] * (o_nsa @ w_a) + gm[:, :, 1] * (y_ret @ w_b)) @ w_o


def hier_moe(h, rg_w, rg_b, re_w, re_b, w_gate, w_up, w_down):
    B, T, D = h.shape
    x = h.reshape(B * T, D)
    g_logits = (x @ rg_w + rg_b).astype(jnp.float32)
    g_sel = jnp.argmax(g_logits, axis=-1)
    g_w = jnp.take_along_axis(jax.nn.softmax(g_logits, axis=-1), g_sel[:, None], axis=-1)
    e_logits = (x @ re_w + re_b).astype(jnp.float32).reshape(-1, N_GROUPS, EXPERTS_PER_GROUP)
    e_in = jnp.take_along_axis(e_logits, g_sel[:, None, None], axis=1)[:, 0]
    top_v, top_i = lax.top_k(e_in, TOP_K_IN_GROUP)
    w = jax.nn.softmax(top_v, axis=-1) * g_w
    within = jnp.einsum('nk,nke->ne', w, jax.nn.one_hot(top_i, EXPERTS_PER_GROUP, dtype=jnp.float32))
    combine = (jax.nn.one_hot(g_sel, N_GROUPS, dtype=jnp.float32)[:, :, None]
               * within[:, None, :]).astype(x.dtype)
    y = jnp.zeros_like(x)
    for grp in range(N_GROUPS):
        sl = slice(grp * EXPERTS_PER_GROUP, (grp + 1) * EXPERTS_PER_GROUP)
        a = jnp.einsum('nd,edf->nef', x, w_gate[sl])
        u = jnp.einsum('nd,edf->nef', x, w_up[sl])
        y = y + jnp.einsum('nef,efd->nd', jax.nn.silu(a) * u * combine[:, grp, :, None], w_down[sl])
    return y.reshape(B, T, D)


def setup_inputs(seed: int = 0) -> dict:
    key = jax.random.key(seed)
    ks = jax.random.split(key, 28)
    n_pages = PAST_LEN // PAGE_SIZE
    n_phys = (5 * DEC_BATCH * n_pages + 3) // 4
    win_len = min(WINDOW, PAST_LEN)
    kv_row = (2, NSA_KV_HEADS, HEAD_DIM)

    def nrm(k, shape, scale=1.0):
        return scale * jax.random.normal(k, shape, jnp.float32)

    page_table = jax.random.permutation(ks[8], n_phys)[:DEC_BATCH * n_pages].reshape(DEC_BATCH, n_pages).astype(jnp.int32)
    return {
        'x_prompt': nrm(ks[0], (BATCH, SEQ, D_MODEL)),
        'x_sample': nrm(ks[1], (DEC_BATCH, DEC_SEQ, D_MODEL)),
        'c_prompt': nrm(ks[2], (BATCH, D_MODEL)),
        'c_sample': nrm(ks[3], (DEC_BATCH, D_MODEL)),
        'cache_cmp_kv': nrm(ks[4], (DEPTH, n_phys, PAGE_SIZE) + kv_row),
        'cache_slc_kv': nrm(ks[5], (DEPTH, n_phys, PAGE_SIZE) + kv_row),
        'state_win_kv': nrm(ks[6], (DEPTH, DEC_BATCH, win_len) + kv_row),
        'state_ret': nrm(ks[7], (DEPTH, DEC_BATCH, RET_HEADS, RET_DK, RET_DV), 0.5),
        'page_table': page_table,
        'norm_mix_g': 1.0 + nrm(ks[9], (DEPTH, D_MODEL), 0.05),
        'norm_ffn_g': 1.0 + nrm(ks[10], (DEPTH, D_MODEL), 0.05),
        'norm_final_g': 1.0 + nrm(ks[11], (D_MODEL,), 0.05),
        'w_ada': nrm(ks[12], (DEPTH, D_MODEL, N_MOD * D_MODEL), 0.5 * D_MODEL ** -0.5),
        'b_ada': nrm(ks[13], (DEPTH, N_MOD * D_MODEL), 0.02),
        'w_in': nrm(ks[14], (DEPTH, D_MODEL, IN_COLS), D_MODEL ** -0.5),
        'cmp_pe': nrm(ks[15], (DEPTH, 2, CMP_BLOCK, HEAD_DIM), 0.1),
        'cmp_phi': nrm(ks[16], (DEPTH, 2, CMP_BLOCK * HEAD_DIM, HEAD_DIM), (CMP_BLOCK * HEAD_DIM) ** -0.5),
        'ret_gn_g': 1.0 + nrm(ks[17], (DEPTH, RET_WIDTH), 0.05),
        'w_branch_nsa': nrm(ks[18], (DEPTH, NSA_WIDTH, D_MODEL), NSA_WIDTH ** -0.5),
        'w_branch_ret': nrm(ks[19], (DEPTH, RET_WIDTH, D_MODEL), RET_WIDTH ** -0.5),
        'w_out': nrm(ks[20], (DEPTH, D_MODEL, D_MODEL), D_MODEL ** -0.5),
        'router_group_w': nrm(ks[21], (DEPTH, D_MODEL, N_GROUPS), D_MODEL ** -0.5),
        'router_group_b': nrm(ks[22], (DEPTH, N_GROUPS), 0.01),
        'router_expert_w': nrm(ks[23], (DEPTH, D_MODEL, N_EXPERTS), D_MODEL ** -0.5),
        'router_expert_b': nrm(ks[24], (DEPTH, N_EXPERTS), 0.01),
        'expert_w_gate': nrm(ks[25], (DEPTH, N_EXPERTS, D_MODEL, D_EXPERT), D_MODEL ** -0.5),
        'expert_w_up': nrm(ks[26], (DEPTH, N_EXPERTS, D_MODEL, D_EXPERT), D_MODEL ** -0.5),
        'expert_w_down': nrm(ks[27], (DEPTH, N_EXPERTS, D_EXPERT, D_MODEL), D_EXPERT ** -0.5),
    }


def reference(x_prompt, x_sample, c_prompt, c_sample, cache_cmp_kv, cache_slc_kv, state_win_kv, state_ret,
              page_table, norm_mix_g, norm_ffn_g, norm_final_g, w_ada, b_ada, w_in, cmp_pe, cmp_phi, ret_gn_g,
              w_branch_nsa, w_branch_ret, w_out, router_group_w, router_group_b, router_expert_w,
              router_expert_b, expert_w_gate, expert_w_up, expert_w_down):
    slopes = alibi_slopes()
    xp, xs = x_prompt, x_sample
    B, T = xp.shape[:2]
    win_p = min(WINDOW, T)
    cmp_p, slc_p, win_pl, ret_p = [], [], [], []
    cmp_s, slc_s, win_sl, ret_s = [], [], [], []
    for l in range(DEPTH):
        moe_w = (router_group_w[l], router_group_b[l], router_expert_w[l], router_expert_b[l],
                 expert_w_gate[l], expert_w_up[l], expert_w_down[l])
        sh_a, sc_a, gt_a, sh_m, sc_m, gt_m = adaln(c_prompt, w_ada[l], b_ada[l])
        h = rmsnorm(xp, norm_mix_g[l]) * (1 + sc_a) + sh_a
        q, kv, g, rq, rk, rv, rg, gm = split_proj(h @ w_in[l])
        o_nsa = nsa_prompt(q, kv, g, cmp_pe[l], cmp_phi[l], slopes)
        s0 = jnp.zeros((B, RET_HEADS, RET_DK, RET_DV), jnp.float32)
        y_ret, s_new = retention_branch(rq, rk, rv, rg, s0, ret_gn_g[l])
        xp = xp + gt_a * merge_branches(o_nsa, y_ret, gm, w_branch_nsa[l], w_branch_ret[l], w_out[l])
        h = rmsnorm(xp, norm_ffn_g[l]) * (1 + sc_m) + sh_m
        xp = xp + gt_m * hier_moe(h, *moe_w)
        cmp_p.append(kv[:, :, 0])
        slc_p.append(kv[:, :, 1])
        win_pl.append(kv[:, T - win_p:, 2])
        ret_p.append(s_new.astype(state_ret.dtype))
        sh_a, sc_a, gt_a, sh_m, sc_m, gt_m = adaln(c_sample, w_ada[l], b_ada[l])
        h = rmsnorm(xs, norm_mix_g[l]) * (1 + sc_a) + sh_a
        q, kv, g, rq, rk, rv, rg, gm = split_proj(h @ w_in[l])
        o_nsa, win_new = nsa_sample(q, kv, g, cache_cmp_kv[l], cache_slc_kv[l], state_win_kv[l], page_table,
                                    cmp_pe[l], cmp_phi[l], slopes)
        y_ret, s_new = retention_branch(rq, rk, rv, rg, state_ret[l], ret_gn_g[l])
        xs = xs + gt_a * merge_branches(o_nsa, y_ret, gm, w_branch_nsa[l], w_branch_ret[l], w_out[l])
        h = rmsnorm(xs, norm_ffn_g[l]) * (1 + sc_m) + sh_m
        xs = xs + gt_m * hier_moe(h, *moe_w)
        cmp_s.append(kv[:, :, 0])
        slc_s.append(kv[:, :, 1])
        win_sl.append(win_new)
        ret_s.append(s_new.astype(state_ret.dtype))
    y_prompt = rmsnorm(xp, norm_final_g)
    y_sample = rmsnorm(xs, norm_final_g)
    new_cmp_kv_prompt = jnp.stack(cmp_p)
    new_slc_kv_prompt = jnp.stack(slc_p)
    new_win_kv_prompt = jnp.stack(win_pl)
    new_ret_prompt = jnp.stack(ret_p)
    new_cmp_kv_sample = jnp.stack(cmp_s)
    new_slc_kv_sample = jnp.stack(slc_s)
    new_win_kv_sample = jnp.stack(win_sl)
    new_ret_sample = jnp.stack(ret_s)
    return (y_prompt, y_sample, new_cmp_kv_prompt, new_slc_kv_prompt, new_win_kv_prompt, new_ret_prompt,
            new_cmp_kv_sample, new_slc_kv_sample, new_win_kv_sample, new_ret_sample)
```

```python
import functools
import math

import numpy as np
import jax
import jax.numpy as jnp
from jax import lax
from jax.experimental import pallas as pl
from jax.experimental.pallas import tpu as pltpu

D_MODEL = 2048
DEPTH = 2
NSA_HEADS = 8
NSA_KV_HEADS = 2
NSA_GROUP = NSA_HEADS // NSA_KV_HEADS
HEAD_DIM = 128
NSA_WIDTH = NSA_HEADS * HEAD_DIM
CMP_STRIDE = 16
CMP_BLOCK = 2 * CMP_STRIDE
SEL_BLOCK = 64
SEG_PER_SEL = SEL_BLOCK // CMP_STRIDE
N_SELECT = 16
N_LOCAL_SEL = 2
WINDOW = 512
SCALE = HEAD_DIM ** -0.5
RET_HEADS = 4
RET_DK = 256
RET_DV = 256
RET_QK = RET_HEADS * RET_DK
RET_WIDTH = RET_HEADS * RET_DV
RET_CHUNK = 128
N_GROUPS = 4
EXPERTS_PER_GROUP = 8
N_EXPERTS = N_GROUPS * EXPERTS_PER_GROUP
TOP_K_IN_GROUP = 2
D_EXPERT = 1024
N_MOD = 6
EPS = 1e-6
NEG_INF = -1e30
KV_COLS = 3 * 2 * NSA_KV_HEADS * HEAD_DIM
OFF_NSA_KV = NSA_WIDTH
OFF_NSA_G = OFF_NSA_KV + KV_COLS
OFF_RET = OFF_NSA_G + 3 * NSA_HEADS
OFF_MERGE = OFF_RET + 2 * RET_QK + 2 * RET_WIDTH
IN_COLS = OFF_MERGE + 2 * D_MODEL

LANES = 128
VMEM_LIMIT_BYTES = 56 * 1024 * 1024

ALIBI = tuple(tuple(2.0 ** -(g * NSA_GROUP + r + 1) for r in range(NSA_GROUP)) for g in range(NSA_KV_HEADS))


def _cparams(*sem):
    return pltpu.CompilerParams(dimension_semantics=sem, vmem_limit_bytes=VMEM_LIMIT_BYTES)


def _norm_mod_kernel(x_ref, g_ref, sc_ref, sh_ref, o_ref):
    x = x_ref[...]
    y = x * lax.rsqrt(jnp.mean(x * x, axis=-1, keepdims=True) + EPS)
    y = y * g_ref[...]
    h = y * (1.0 + sc_ref[...]) + sh_ref[...]
    tb, tt, d = x.shape
    o_ref[...] = h.reshape(tb * tt, d).astype(o_ref.dtype)


def _norm_mod(x, g, scale, shift, *, tb, tt):
    B, T, D = x.shape
    return pl.pallas_call(
        _norm_mod_kernel,
        out_shape=jax.ShapeDtypeStruct((B * T, D), jnp.bfloat16),
        grid=(B // tb, T // tt),
        in_specs=[
            pl.BlockSpec((tb, tt, D), lambda b, t: (b, t, 0)),
            pl.BlockSpec((1, 1, D), lambda b, t: (0, 0, 0)),
            pl.BlockSpec((tb, 1, D), lambda b, t: (b, 0, 0)),
            pl.BlockSpec((tb, 1, D), lambda b, t: (b, 0, 0)),
        ],
        out_specs=pl.BlockSpec((tb * tt, D), lambda b, t: (b * (T // tt) + t, 0)),
        compiler_params=_cparams("parallel", "parallel"),
        name="norm_mod",
    )(x, g.reshape(1, 1, D), scale, shift)


def _final_norm_kernel(x_ref, g_ref, o_ref):
    x = x_ref[...]
    y = x * lax.rsqrt(jnp.mean(x * x, axis=-1, keepdims=True) + EPS)
    o_ref[...] = y * g_ref[...]


def _final_norm(x2, g, *, tm):
    M, D = x2.shape
    return pl.pallas_call(
        _final_norm_kernel,
        out_shape=jax.ShapeDtypeStruct((M, D), jnp.float32),
        grid=(M // tm,),
        in_specs=[pl.BlockSpec((tm, D), lambda i: (i, 0)), pl.BlockSpec((1, D), lambda i: (0, 0))],
        out_specs=pl.BlockSpec((tm, D), lambda i: (i, 0)),
        compiler_params=_cparams("parallel"),
        name="final_norm",
    )(x2, g.reshape(1, D))


def _mm_kernel(x_ref, w_ref, o_ref):
    acc = jnp.dot(x_ref[...].astype(jnp.bfloat16), w_ref[...].astype(jnp.bfloat16),
                  preferred_element_type=jnp.float32)
    o_ref[...] = acc.astype(o_ref.dtype)


def _mm(x, w, *, tm, tn, out_dtype=jnp.float32, name="mm"):
    M, K = x.shape
    N = w.shape[1]
    return pl.pallas_call(
        _mm_kernel,
        out_shape=jax.ShapeDtypeStruct((M, N), out_dtype),
        grid=(M // tm, N // tn),
        in_specs=[pl.BlockSpec((tm, K), lambda i, j: (i, 0)), pl.BlockSpec((K, tn), lambda i, j: (0, j))],
        out_specs=pl.BlockSpec((tm, tn), lambda i, j: (i, j)),
        compiler_params=_cparams("parallel", "parallel"),
        name=name,
    )(x, w)


def _merge_kernel(a_ref, b_ref, g0_ref, g1_ref, wa_ref, wb_ref, o_ref):
    pa = jnp.dot(a_ref[...].astype(jnp.bfloat16), wa_ref[...].astype(jnp.bfloat16),
                 preferred_element_type=jnp.float32)
    pb = jnp.dot(b_ref[...].astype(jnp.bfloat16), wb_ref[...].astype(jnp.bfloat16),
                 preferred_element_type=jnp.float32)
    o_ref[...] = (jax.nn.sigmoid(g0_ref[...]) * pa + jax.nn.sigmoid(g1_ref[...]) * pb).astype(o_ref.dtype)


def _merge(o_nsa, y_ret, zgm, w_a, w_b, *, tm, tn):
    M, K = o_nsa.shape
    N = w_a.shape[1]
    nj = N // tn
    return pl.pallas_call(
        _merge_kernel,
        out_shape=jax.ShapeDtypeStruct((M, N), jnp.bfloat16),
        grid=(M // tm, nj),
        in_specs=[
            pl.BlockSpec((tm, K), lambda i, j: (i, 0)),
            pl.BlockSpec((tm, K), lambda i, j: (i, 0)),
            pl.BlockSpec((tm, tn), lambda i, j: (i, j)),
            pl.BlockSpec((tm, tn), lambda i, j: (i, j + nj)),
            pl.BlockSpec((K, tn), lambda i, j: (0, j)),
            pl.BlockSpec((K, tn), lambda i, j: (0, j)),
        ],
        out_specs=pl.BlockSpec((tm, tn), lambda i, j: (i, j)),
        compiler_params=_cparams("parallel", "parallel"),
        name="merge_branches",
    )(o_nsa, y_ret, zgm, zgm, w_a, w_b)


def _resid_mm_kernel(t_ref, w_ref, x_ref, gt_ref, o_ref):
    acc = jnp.dot(t_ref[...], w_ref[...].astype(jnp.bfloat16), preferred_element_type=jnp.float32)
    tb, tt, tn = x_ref.shape
    o_ref[...] = x_ref[...] + gt_ref[...] * acc.reshape(tb, tt, tn)


def _resid_mm(t, w, x, gate, *, tb, tt, tn):
    B, T, N = x.shape
    K = t.shape[1]
    nt = T // tt
    return pl.pallas_call(
        _resid_mm_kernel,
        out_shape=jax.ShapeDtypeStruct((B, T, N), jnp.float32),
        grid=(B // tb, nt, N // tn),
        in_specs=[
            pl.BlockSpec((tb * tt, K), lambda b, t_, j: (b * nt + t_, 0)),
            pl.BlockSpec((K, tn), lambda b, t_, j: (0, j)),
            pl.BlockSpec((tb, tt, tn), lambda b, t_, j: (b, t_, j)),
            pl.BlockSpec((tb, 1, tn), lambda b, t_, j: (b, 0, j)),
        ],
        out_specs=pl.BlockSpec((tb, tt, tn), lambda b, t_, j: (b, t_, j)),
        compiler_params=_cparams("parallel", "parallel", "parallel"),
        name="resid_out_proj",
    )(t, w, x, gate)


def _sel_matrix(nb, ns):
    i = np.arange(nb)[:, None]
    j = np.arange(ns)[None, :]
    lo, hi = SEG_PER_SEL * j, SEG_PER_SEL * j + SEG_PER_SEL - 1
    m = ((i >= lo) & (i <= hi)).astype(np.float32) + ((i + 1 >= lo) & (i + 1 <= hi)).astype(np.float32)
    return jnp.asarray(m, jnp.bfloat16)


def _cmp_sel_kernel(q_ref, kc_ref, vc_ref, msel_ref, o_ref, sel_ref, *, tq, nb, ns):
    g = pl.program_id(0)
    qi = pl.program_id(1)
    qpos = qi * tq + lax.broadcasted_iota(jnp.int32, (tq, 1), 0)
    endpos = lax.broadcasted_iota(jnp.int32, (1, nb), 1) * CMP_STRIDE + (CMP_BLOCK - 1)
    dist = qpos - endpos
    mask = dist >= 0
    distf = dist.astype(jnp.float32)
    kc = kc_ref[0]
    vc = vc_ref[0]
    imp = jnp.zeros((tq, nb), jnp.float32)
    for r in range(NSA_GROUP):
        slope = jnp.where(g == 0, ALIBI[0][r], ALIBI[1][r])
        s = lax.dot_general(q_ref[:, r * HEAD_DIM:(r + 1) * HEAD_DIM], kc, (((1,), (1,)), ((), ())),
                            preferred_element_type=jnp.float32)
        s = s * SCALE - slope * distf
        s = jnp.where(mask, s, NEG_INF)
        m = jnp.max(s, axis=-1, keepdims=True)
        e = jnp.exp(s - m)
        p = jnp.where(mask, e * (1.0 / jnp.sum(e, axis=-1, keepdims=True)), 0.0)
        o_ref[:, r * HEAD_DIM:(r + 1) * HEAD_DIM] = jnp.dot(p.astype(jnp.bfloat16), vc,
                                                            preferred_element_type=jnp.float32)
        imp = imp + p
    msel = msel_ref[...]
    hi = imp.astype(jnp.bfloat16)
    r1 = imp - hi.astype(jnp.float32)
    mid = r1.astype(jnp.bfloat16)
    lo = (r1 - mid.astype(jnp.float32)).astype(jnp.bfloat16)
    sel = (jnp.dot(hi, msel, preferred_element_type=jnp.float32)
           + jnp.dot(mid, msel, preferred_element_type=jnp.float32)
           + jnp.dot(lo, msel, preferred_element_type=jnp.float32))
    jf = lax.broadcasted_iota(jnp.int32, (1, ns), 1)
    cur = qpos // SEL_BLOCK
    valid = jf <= cur
    forced = (jf == 0) | (valid & (jf > cur - N_LOCAL_SEL))
    score = jnp.where(forced, 1e9, jnp.where(valid, sel, -1e9))
    lane = jf.astype(jnp.float32)
    chosen = jnp.zeros((tq, ns), jnp.float32)
    for _ in range(min(N_SELECT, ns)):
        m = jnp.max(score, axis=-1, keepdims=True)
        first = jnp.min(jnp.where(score == m, lane, float(ns)), axis=-1, keepdims=True)
        hit = lane == first
        chosen = jnp.where(hit, 1.0, chosen)
        score = jnp.where(hit, -3e38, score)
    sel_ref[0] = jnp.where(valid, chosen, 0.0).astype(sel_ref.dtype)


def _cmp_sel(qb, kc, vc, *, tq):
    T = qb.shape[0]
    nb = kc.shape[1]
    ns = T // SEL_BLOCK
    gw = NSA_GROUP * HEAD_DIM
    return pl.pallas_call(
        functools.partial(_cmp_sel_kernel, tq=tq, nb=nb, ns=ns),
        out_shape=(jax.ShapeDtypeStruct((T, NSA_WIDTH), jnp.float32),
                   jax.ShapeDtypeStruct((NSA_KV_HEADS, T, ns), jnp.bfloat16)),
        grid=(NSA_KV_HEADS, T // tq),
        in_specs=[
            pl.BlockSpec((tq, gw), lambda g, i: (i, g)),
            pl.BlockSpec((1, nb, HEAD_DIM), lambda g, i: (g, 0, 0)),
            pl.BlockSpec((1, nb, HEAD_DIM), lambda g, i: (g, 0, 0)),
            pl.BlockSpec((nb, ns), lambda g, i: (0, 0)),
        ],
        out_specs=(pl.BlockSpec((tq, gw), lambda g, i: (i, g)),
                   pl.BlockSpec((1, tq, ns), lambda g, i: (g, i, 0))),
        compiler_params=_cparams("parallel", "parallel"),
        name="nsa_cmp_select",
    )(qb, kc, vc, _sel_matrix(nb, ns))


def _flash_kernel(*refs, tq, tk, nk, mode):
    if mode == "sel":
        q_ref, k_ref, v_ref, sel_ref, o_ref, m_scr, l_scr, acc_scr = refs
    else:
        q_ref, k_ref, v_ref, o_ref, m_scr, l_scr, acc_scr = refs
    g = pl.program_id(0)
    qi = pl.program_id(1)
    kj = pl.program_id(2)
    if mode == "sel":
        kt = kj
        valid = kj * tk <= qi * tq + (tq - 1)
    else:
        kt = qi * (tq // tk) - WINDOW // tk + kj
        valid = kt >= 0

    @pl.when(kj == 0)
    def _():
        m_scr[...] = jnp.full(m_scr.shape, NEG_INF, jnp.float32)
        l_scr[...] = jnp.zeros(l_scr.shape, jnp.float32)
        acc_scr[...] = jnp.zeros(acc_scr.shape, jnp.float32)

    @pl.when(valid)
    def _():
        k = k_ref[...]
        v = v_ref[...]
        kpos = kt * tk + lax.broadcasted_iota(jnp.int32, (1, tk), 1)
        dist = (qi * tq + lax.broadcasted_iota(jnp.int32, (tq, 1), 0)) - kpos
        if mode == "sel":
            blk = lax.broadcasted_iota(jnp.int32, (sel_ref.shape[2], 1), 0)
            expand = (kpos // SEL_BLOCK == blk).astype(jnp.bfloat16)
            picked = jnp.dot(sel_ref[0], expand, preferred_element_type=jnp.float32)
            msk = (picked > 0.5) & (dist >= 0)
        else:
            msk = (dist >= 0) & (dist < WINDOW)
        distf = dist.astype(jnp.float32)
        for r in range(NSA_GROUP):
            slope = jnp.where(g == 0, ALIBI[0][r], ALIBI[1][r])
            s = lax.dot_general(q_ref[:, r * HEAD_DIM:(r + 1) * HEAD_DIM], k, (((1,), (1,)), ((), ())),
                                preferred_element_type=jnp.float32)
            s = jnp.where(msk, s * SCALE - slope * distf, NEG_INF)
            m_prev = m_scr[r]
            m_new = jnp.maximum(m_prev, jnp.max(s, axis=-1, keepdims=True))
            alpha = jnp.exp(m_prev - m_new)
            p = jnp.where(msk, jnp.exp(s - m_new), 0.0)
            l_scr[r] = alpha * l_scr[r] + jnp.sum(p, axis=-1, keepdims=True)
            acc_scr[r] = alpha * acc_scr[r] + jnp.dot(p.astype(jnp.bfloat16), v,
                                                      preferred_element_type=jnp.float32)
            m_scr[r] = m_new

    @pl.when(kj == nk - 1)
    def _():
        for r in range(NSA_GROUP):
            o_ref[:, r * HEAD_DIM:(r + 1) * HEAD_DIM] = acc_scr[r] * (1.0 / l_scr[r])


def _flash(qb, kvb, selm, *, branch, mode, tq, tk):
    T = qb.shape[0]
    gw = NSA_GROUP * HEAD_DIM
    kcol = (branch * 2 + 0) * NSA_KV_HEADS
    vcol = (branch * 2 + 1) * NSA_KV_HEADS
    if mode == "sel":
        nk = T // tk

        def ktile(i, j):
            return jnp.minimum(j, (i * tq + tq - 1) // tk)
    else:
        nk = (WINDOW + tq) // tk

        def ktile(i, j):
            return jnp.maximum(i * (tq // tk) - WINDOW // tk + j, 0)

    in_specs = [
        pl.BlockSpec((tq, gw), lambda g, i, j: (i, g)),
        pl.BlockSpec((tk, HEAD_DIM), lambda g, i, j: (ktile(i, j), kcol + g)),
        pl.BlockSpec((tk, HEAD_DIM), lambda g, i, j: (ktile(i, j), vcol + g)),
    ]
    args = [qb, kvb, kvb]
    if mode == "sel":
        ns = selm.shape[2]
        in_specs.append(pl.BlockSpec((1, tq, ns), lambda g, i, j: (g, i, 0)))
        args.append(selm)
    return pl.pallas_call(
        functools.partial(_flash_kernel, tq=tq, tk=tk, nk=nk, mode=mode),
        out_shape=jax.ShapeDtypeStruct((T, NSA_WIDTH), jnp.float32),
        grid=(NSA_KV_HEADS, T // tq, nk),
        in_specs=in_specs,
        out_specs=pl.BlockSpec((tq, gw), lambda g, i, j: (i, g)),
        scratch_shapes=[pltpu.VMEM((NSA_GROUP, tq, 1), jnp.float32),
                        pltpu.VMEM((NSA_GROUP, tq, 1), jnp.float32),
                        pltpu.VMEM((NSA_GROUP, tq, HEAD_DIM), jnp.float32)],
        compiler_params=_cparams("parallel", "parallel", "arbitrary"),
        name="nsa_" + mode + "_attn",
    )(*args)


def _moe_kernel(tile_ref, exp_ref, lo_ref, hi_ref, first_ref, valid_ref,
                xs_ref, ws_ref, wg_ref, wu_ref, wd_ref, o_ref, *, tm):
    k = pl.program_id(0)
    f = pl.program_id(1)

    @pl.when((first_ref[k] == 1) & (f == 0))
    def _():
        o_ref[...] = jnp.zeros(o_ref.shape, jnp.float32)

    @pl.when(valid_ref[k] == 1)
    def _():
        x = xs_ref[...]
        a = jnp.dot(x, wg_ref[0].astype(jnp.bfloat16), preferred_element_type=jnp.float32)
        u = jnp.dot(x, wu_ref[0].astype(jnp.bfloat16), preferred_element_type=jnp.float32)
        rows = tile_ref[k] * tm + lax.broadcasted_iota(jnp.int32, (tm, 1), 0)
        mine = (rows >= lo_ref[k]) & (rows < hi_ref[k])
        h = jnp.where(mine, a * jax.nn.sigmoid(a) * u * ws_ref[...], 0.0)
        o_ref[...] += jnp.dot(h.astype(jnp.bfloat16), wd_ref[0].astype(jnp.bfloat16),
                              preferred_element_type=jnp.float32)


def _moe_experts(xs, ws, meta, w_gate, w_up, w_down, *, tm, tf):
    R, D = xs.shape
    F = w_gate.shape[2]
    nf = F // tf
    n_items = meta[0].shape[0]

    def fidx(k, f, valid):
        return jnp.where(valid[k] == 1, f, nf - 1)

    grid_spec = pltpu.PrefetchScalarGridSpec(
        num_scalar_prefetch=6,
        grid=(n_items, nf),
        in_specs=[
            pl.BlockSpec((tm, D), lambda k, f, t, e, lo, hi, fi, va: (t[k], 0)),
            pl.BlockSpec((tm, 1), lambda k, f, t, e, lo, hi, fi, va: (t[k], 0)),
            pl.BlockSpec((1, D, tf), lambda k, f, t, e, lo, hi, fi, va: (e[k], 0, fidx(k, f, va))),
            pl.BlockSpec((1, D, tf), lambda k, f, t, e, lo, hi, fi, va: (e[k], 0, fidx(k, f, va))),
            pl.BlockSpec((1, tf, D), lambda k, f, t, e, lo, hi, fi, va: (e[k], fidx(k, f, va), 0)),
        ],
        out_specs=pl.BlockSpec((tm, D), lambda k, f, t, e, lo, hi, fi, va: (t[k], 0)),
    )
    return pl.pallas_call(
        functools.partial(_moe_kernel, tm=tm),
        out_shape=jax.ShapeDtypeStruct((R, D), jnp.float32),
        grid_spec=grid_spec,
        compiler_params=_cparams("arbitrary", "arbitrary"),
        name="moe_experts",
    )(*meta, xs, ws, w_gate, w_up, w_down)


def _moe_meta(flat_e, *, tm, n_tiles):
    n_items = n_tiles + N_EXPERTS - 1
    sizes = jnp.zeros((N_EXPERTS,), jnp.int32).at[flat_e].add(1)
    ends = jnp.cumsum(sizes)
    starts = ends - sizes
    first_tile = starts // tm
    last_tile = jnp.maximum(ends - 1, 0) // tm
    cnt = jnp.where(sizes > 0, last_tile - first_tile + 1, 0)
    cum = jnp.cumsum(cnt)
    total = cum[-1]
    k = jnp.arange(n_items, dtype=jnp.int32)
    kk = jnp.minimum(k, total - 1)
    e = jnp.searchsorted(cum, kk, side="right").astype(jnp.int32)
    tile = first_tile[e] + (kk - (cum[e] - cnt[e]))
    valid = (k < total).astype(jnp.int32)
    first = jnp.concatenate([jnp.ones((1,), jnp.int32), (tile[1:] != tile[:-1]).astype(jnp.int32)]) * valid
    return (tile.astype(jnp.int32), e, starts[e].astype(jnp.int32), ends[e].astype(jnp.int32), first, valid)


def _hier_moe(hb, rg_w, rg_b, re_w, re_b, w_gate, w_up, w_down, *, tm=512, tf=256):
    N, D = hb.shape
    n_route = N_GROUPS + N_EXPERTS
    w_route = jnp.pad(jnp.concatenate([rg_w, re_w], axis=1), ((0, 0), (0, LANES - n_route)))
    logits = _mm(hb, w_route, tm=1024, tn=LANES, name="moe_router")
    g_logits = logits[:, :N_GROUPS] + rg_b
    e_logits = (logits[:, N_GROUPS:n_route] + re_b).reshape(N, N_GROUPS, EXPERTS_PER_GROUP)
    g_sel = jnp.argmax(g_logits, axis=-1)
    g_w = jnp.take_along_axis(jax.nn.softmax(g_logits, axis=-1), g_sel[:, None], axis=-1)
    e_in = jnp.take_along_axis(e_logits, g_sel[:, None, None], axis=1)[:, 0]
    top_v, top_i = lax.top_k(e_in, TOP_K_IN_GROUP)
    w = jax.nn.softmax(top_v, axis=-1) * g_w
    flat_e = (g_sel[:, None] * EXPERTS_PER_GROUP + top_i).reshape(-1).astype(jnp.int32)
    order = jnp.argsort(flat_e, stable=True)
    tok = order // TOP_K_IN_GROUP
    xs = hb[tok]
    ws = w.reshape(-1)[order][:, None]
    R = N * TOP_K_IN_GROUP
    meta = _moe_meta(flat_e, tm=tm, n_tiles=R // tm)
    ys = _moe_experts(xs, ws, meta, w_gate, w_up, w_down, tm=tm, tf=tf)
    inv = jnp.argsort(order)
    return ys[inv].reshape(N, TOP_K_IN_GROUP, D).sum(axis=1)


def _masked_softmax(logits, mask, axis=-1):
    p = jax.nn.softmax(jnp.where(mask, logits, NEG_INF), axis=axis)
    return jnp.where(mask, p, 0.0)


def _compress_rows(rows, pe, phi):
    B, L = rows.shape[:2]
    n_seg = L // CMP_STRIDE
    seg = rows.reshape(B, n_seg, CMP_STRIDE, NSA_KV_HEADS, HEAD_DIM)
    nxt = jnp.concatenate([seg[:, 1:], jnp.zeros_like(seg[:, :1])], axis=1)
    blk = jnp.concatenate([seg, nxt], axis=2) + pe[None, None, :, None, :]
    return jnp.einsum('bnlgd,lde->bnge', blk, phi.reshape(CMP_BLOCK, HEAD_DIM, HEAD_DIM))


def _compressed_attn(q, kc, vc, q_pos, slopes):
    n_blk = kc.shape[1]
    end = jnp.arange(n_blk) * CMP_STRIDE + (CMP_BLOCK - 1)
    dist = q_pos[:, None] - end[None, :]
    s = (jnp.einsum('bqgrd,bngd->bgrqn', q, kc).astype(jnp.float32) * SCALE
         - slopes[None, :, :, None, None] * dist.astype(jnp.float32))
    p = _masked_softmax(s, dist >= 0)
    o = jnp.einsum('bgrqn,bngd->bqgrd', p, vc)
    return o, p


def _select_mask(p_cmp, q_pos):
    imp = p_cmp.sum(axis=2)
    seg = imp + jnp.pad(imp[..., :-1], ((0, 0), (0, 0), (0, 0), (1, 0)))
    n_sel = imp.shape[-1] // SEG_PER_SEL
    sel = seg.reshape(seg.shape[:-1] + (n_sel, SEG_PER_SEL)).sum(-1)
    j = jnp.arange(n_sel)[None, :]
    cur = (q_pos // SEL_BLOCK)[:, None]
    valid = j <= cur
    forced = (j == 0) | (valid & (j > cur - N_LOCAL_SEL))
    score = jnp.where(forced, 1e9, jnp.where(valid, sel, -1e9))
    _, idx = lax.top_k(score, min(N_SELECT, n_sel))
    return jnp.any(idx[..., None] == jnp.arange(n_sel), axis=-2)


def _window_attn(q, k, v, dist, slopes):
    mask = (dist >= 0) & (dist < WINDOW)
    s = (jnp.einsum('bqgrd,bkgd->bgrqk', q, k).astype(jnp.float32) * SCALE
         - slopes[None, :, :, None, None] * dist.astype(jnp.float32))
    p = _masked_softmax(s, mask)
    return jnp.einsum('bgrqk,bkgd->bqgrd', p, v)


def _combine_nsa(g, o_cmp, o_slc, o_win):
    B, T = g.shape[:2]
    o = (g[:, :, 0, :, :, None] * o_cmp + g[:, :, 1, :, :, None] * o_slc
         + g[:, :, 2, :, :, None] * o_win)
    return o.reshape(B, T, NSA_WIDTH)


def _nsa_sample(q, kv, g, cmp_pages, slc_pages, win_buf, page_table, pe, phi, slopes):
    Bd, Tq = q.shape[:2]
    past = page_table.shape[1] * cmp_pages.shape[1]
    total = past + Tq
    l_pad = -(-total // SEL_BLOCK) * SEL_BLOCK
    q_pos = past + jnp.arange(Tq)

    def full_rows(pages, new):
        flat = pages.reshape(pages.shape[0], -1)
        old = jnp.take(flat, page_table.reshape(-1), axis=0).reshape((Bd, past) + pages.shape[2:])
        rows = jnp.concatenate([old, new.astype(old.dtype)], axis=1)
        return jnp.pad(rows, ((0, 0), (0, l_pad - total), (0, 0), (0, 0), (0, 0)))

    cmp_rows = full_rows(cmp_pages, kv[:, :, 0])
    slc_rows = full_rows(slc_pages, kv[:, :, 1])
    kc = _compress_rows(cmp_rows[:, :, 0], pe[0], phi[0])
    vc = _compress_rows(cmp_rows[:, :, 1], pe[1], phi[1])
    o_cmp, p_cmp = _compressed_attn(q, kc, vc, q_pos, slopes)
    selm = _select_mask(p_cmp, q_pos)
    pos = jnp.arange(l_pad)
    dist = q_pos[:, None] - pos[None, :]
    keep = jnp.repeat(selm, SEL_BLOCK, axis=-1) & (dist >= 0)[None, None]
    s = (jnp.einsum('bqgrd,bkgd->bgrqk', q, slc_rows[:, :, 0]).astype(jnp.float32) * SCALE
         - slopes[None, :, :, None, None] * dist.astype(jnp.float32))
    p = _masked_softmax(s, keep[:, :, None])
    o_slc = jnp.einsum('bgrqk,bkgd->bqgrd', p, slc_rows[:, :, 1])
    wb = win_buf.shape[1]
    win_rows = jnp.concatenate([win_buf, kv[:, :, 2].astype(win_buf.dtype)], axis=1)
    k_pos = past - wb + jnp.arange(wb + Tq)
    distw = q_pos[:, None] - k_pos[None, :]
    o_win = _window_attn(q, win_rows[:, :, 0], win_rows[:, :, 1], distw, slopes)
    return _combine_nsa(g, o_cmp, o_slc, o_win), win_rows[:, Tq:]


def _retention_chunked(q, k, v, s0, chunk):
    B, T = q.shape[:2]
    nc = T // chunk
    log_gamma = jnp.log1p(-jnp.exp2(-5.0 - jnp.arange(RET_HEADS, dtype=jnp.float32)))
    i = jnp.arange(chunk, dtype=jnp.float32)
    diff = i[:, None] - i[None, :]
    decay_in = jnp.where(diff >= 0, jnp.exp(jnp.maximum(diff, 0.0)[None] * log_gamma[:, None, None]), 0.0)
    decay_q = jnp.exp((i + 1.0)[None, :] * log_gamma[:, None])
    decay_k = jnp.exp((chunk - 1.0 - i)[None, :] * log_gamma[:, None])
    decay_c = jnp.exp(chunk * log_gamma)

    def to_chunks(a):
        return a.astype(jnp.float32).reshape(B, nc, chunk, RET_HEADS, -1).transpose(1, 0, 3, 2, 4)

    def step(s, inp):
        qc, kc, vc = inp
        att = jnp.einsum('bhqd,bhkd->bhqk', qc, kc) * decay_in
        o = (jnp.einsum('bhqk,bhkv->bhqv', att, vc)
             + jnp.einsum('bhqd,bhdv->bhqv', qc, s) * decay_q[None, :, :, None])
        s = s * decay_c[None, :, None, None] + jnp.einsum('bhkd,bhkv->bhdv', kc * decay_k[None, :, :, None], vc)
        return s, o

    s, o = lax.scan(step, s0.astype(jnp.float32), (to_chunks(q), to_chunks(k), to_chunks(v)))
    return o.transpose(1, 0, 3, 2, 4).reshape(B, T, RET_HEADS, RET_DV), s


def _retention_branch(rq, rk, rv, rg, s0, gn_g):
    B, T = rq.shape[:2]
    o, s = _retention_chunked(rq, rk * RET_DK ** -0.5, rv, s0, math.gcd(RET_CHUNK, T))
    mu = o.mean(-1, keepdims=True)
    var = jnp.mean(jnp.square(o - mu), -1, keepdims=True)
    y = ((o - mu) * lax.rsqrt(var + EPS)).reshape(B, T, RET_WIDTH) * gn_g.astype(jnp.float32)
    return (jax.nn.silu(rg.astype(jnp.float32)) * y).astype(rg.dtype), s


def _project(hb, w_in_l, *, tm):
    q = _mm(hb, w_in_l[:, :OFF_NSA_KV], tm=tm, tn=512, out_dtype=jnp.bfloat16, name="proj_q")
    kv = _mm(hb, w_in_l[:, OFF_NSA_KV:OFF_NSA_G], tm=tm, tn=512, name="proj_kv")
    w_g = jnp.pad(w_in_l[:, OFF_NSA_G:OFF_RET], ((0, 0), (0, LANES - 3 * NSA_HEADS)))
    zg = _mm(hb, w_g, tm=tm, tn=LANES, name="proj_gate")
    r = _mm(hb, w_in_l[:, OFF_RET:OFF_MERGE], tm=tm, tn=512, name="proj_ret")
    zgm = _mm(hb, w_in_l[:, OFF_MERGE:], tm=tm, tn=512, name="proj_merge")
    return q, kv, zg, r, zgm


def _nsa_prompt(qb, kv, g_nsa, pe, phi):
    T = qb.shape[0]
    kv6 = kv.reshape(1, T, 3, 2, NSA_KV_HEADS, HEAD_DIM)
    kc = _compress_rows(kv6[:, :, 0, 0], pe[0], phi[0])[0]
    vc = _compress_rows(kv6[:, :, 0, 1], pe[1], phi[1])[0]
    kc = kc.transpose(1, 0, 2).astype(jnp.bfloat16)
    vc = vc.transpose(1, 0, 2).astype(jnp.bfloat16)
    kvb = kv.astype(jnp.bfloat16)
    o_cmp, selm = _cmp_sel(qb, kc, vc, tq=256)
    o_slc = _flash(qb, kvb, selm, branch=1, mode="sel", tq=256, tk=512)
    o_win = _flash(qb, kvb, None, branch=2, mode="win", tq=256, tk=256)
    shp = (1, T, NSA_KV_HEADS, NSA_GROUP, HEAD_DIM)
    return _combine_nsa(g_nsa[None], o_cmp.reshape(shp), o_slc.reshape(shp), o_win.reshape(shp))[0]


def kernel(x_prompt, x_sample, c_prompt, c_sample, cache_cmp_kv, cache_slc_kv, state_win_kv, state_ret, page_table, norm_mix_g, norm_ffn_g, norm_final_g, w_ada, b_ada, w_in, cmp_pe, cmp_phi, ret_gn_g, w_branch_nsa, w_branch_ret, w_out, router_group_w, router_group_b, router_expert_w, router_expert_b, expert_w_gate, expert_w_up, expert_w_down):
    slopes = jnp.asarray(ALIBI, jnp.float32)
    xp, xs = x_prompt, x_sample
    Bp, Tp, D = xp.shape
    Bs, Ts, _ = xs.shape
    Np, Ns = Bp * Tp, Bs * Ts
    win_p = min(WINDOW, Tp)
    n_c = Bp + Bs
    c_rows = -(-n_c // 16) * 16
    c_all = jnp.pad(jax.nn.silu(jnp.concatenate([c_prompt, c_sample], axis=0)), ((0, c_rows - n_c), (0, 0)))
    outs = {k: [] for k in ("cmp_p", "slc_p", "win_p", "ret_p", "cmp_s", "slc_s", "win_s", "ret_s")}
    for l in range(DEPTH):
        mods = (_mm(c_all, w_ada[l], tm=c_rows, tn=1024, name="adaln")[:n_c] + b_ada[l]).reshape(n_c, N_MOD, 1, D)
        mod_p = [mods[:Bp, i] for i in range(N_MOD)]
        mod_s = [mods[Bp:, i] for i in range(N_MOD)]
        hb = _norm_mod(xp, norm_mix_g[l], mod_p[1], mod_p[0], tb=1, tt=512)
        qb, kv, zg, r, zgm = _project(hb, w_in[l], tm=1024)
        g_nsa = jax.nn.sigmoid(zg[:, :3 * NSA_HEADS]).reshape(Np, 3, NSA_KV_HEADS, NSA_GROUP)
        o_nsa = _nsa_prompt(qb, kv, g_nsa, cmp_pe[l], cmp_phi[l])
        r4 = r.reshape(Bp, Tp, 4, RET_QK)
        y_ret, s_new = _retention_branch(
            r4[:, :, 0].reshape(Bp, Tp, RET_HEADS, RET_DK), r4[:, :, 1].reshape(Bp, Tp, RET_HEADS, RET_DK),
            r4[:, :, 2].reshape(Bp, Tp, RET_HEADS, RET_DV), r4[:, :, 3],
            jnp.zeros((Bp, RET_HEADS, RET_DK, RET_DV), jnp.float32), ret_gn_g[l])
        t = _merge(o_nsa, y_ret.reshape(Np, RET_WIDTH), zgm, w_branch_nsa[l], w_branch_ret[l], tm=1024, tn=512)
        xp = _resid_mm(t, w_out[l], xp, mod_p[2], tb=1, tt=1024, tn=512)
        kv6 = kv.reshape(Bp, Tp, 3, 2, NSA_KV_HEADS, HEAD_DIM)
        outs["cmp_p"].append(kv6[:, :, 0])
        outs["slc_p"].append(kv6[:, :, 1])
        outs["win_p"].append(kv6[:, Tp - win_p:, 2])
        outs["ret_p"].append(s_new.astype(state_ret.dtype))
        hb = _norm_mod(xs, norm_mix_g[l], mod_s[1], mod_s[0], tb=32, tt=Ts)
        qb, kv, zg, r, zgm = _project(hb, w_in[l], tm=Ns)
        g_nsa = jax.nn.sigmoid(zg[:, :3 * NSA_HEADS]).reshape(Bs, Ts, 3, NSA_KV_HEADS, NSA_GROUP)
        kv6 = kv.reshape(Bs, Ts, 3, 2, NSA_KV_HEADS, HEAD_DIM)
        q5 = qb.astype(jnp.float32).reshape(Bs, Ts, NSA_KV_HEADS, NSA_GROUP, HEAD_DIM)
        o_nsa, win_new = _nsa_sample(q5, kv6, g_nsa, cache_cmp_kv[l], cache_slc_kv[l], state_win_kv[l],
                                     page_table, cmp_pe[l], cmp_phi[l], slopes)
        r4 = r.reshape(Bs, Ts, 4, RET_QK)
        y_ret, s_new = _retention_branch(
            r4[:, :, 0].reshape(Bs, Ts, RET_HEADS, RET_DK), r4[:, :, 1].reshape(Bs, Ts, RET_HEADS, RET_DK),
            r4[:, :, 2].reshape(Bs, Ts, RET_HEADS, RET_DV), r4[:, :, 3], state_ret[l], ret_gn_g[l])
        t = _merge(o_nsa.reshape(Ns, NSA_WIDTH), y_ret.reshape(Ns, RET_WIDTH), zgm,
                   w_branch_nsa[l], w_branch_ret[l], tm=Ns, tn=512)
        xs = _resid_mm(t, w_out[l], xs, mod_s[2], tb=Bs, tt=Ts, tn=512)
        outs["cmp_s"].append(kv6[:, :, 0])
        outs["slc_s"].append(kv6[:, :, 1])
        outs["win_s"].append(win_new)
        outs["ret_s"].append(s_new.astype(state_ret.dtype))
        hp = _norm_mod(xp, norm_ffn_g[l], mod_p[4], mod_p[3], tb=1, tt=512)
        hs = _norm_mod(xs, norm_ffn_g[l], mod_s[4], mod_s[3], tb=32, tt=Ts)
        y = _hier_moe(jnp.concatenate([hp, hs], axis=0), router_group_w[l], router_group_b[l],
                      router_expert_w[l], router_expert_b[l], expert_w_gate[l], expert_w_up[l], expert_w_down[l])
        xp = xp + mod_p[5] * y[:Np].reshape(Bp, Tp, D)
        xs = xs + mod_s[5] * y[Np:].reshape(Bs, Ts, D)
    y_prompt = _final_norm(xp.reshape(Np, D), norm_final_g, tm=512).reshape(Bp, Tp, D)
    y_sample = _final_norm(xs.reshape(Ns, D), norm_final_g, tm=512).reshape(Bs, Ts, D)
    return (y_prompt, y_sample, jnp.stack(outs["cmp_p"]), jnp.stack(outs["slc_p"]), jnp.stack(outs["win_p"]),
            jnp.stack(outs["ret_p"]), jnp.stack(outs["cmp_s"]), jnp.stack(outs["slc_s"]), jnp.stack(outs["win_s"]),
            jnp.stack(outs["ret_s"]))
```

```python
import functools
import math

import numpy as np
import jax
import jax.numpy as jnp
from jax import lax
from jax.experimental import pallas as pl
from jax.experimental.pallas import tpu as pltpu

D_MODEL = 2048
DEPTH = 2
PAGE_SIZE = 128
NSA_HEADS = 8
NSA_KV_HEADS = 2
NSA_GROUP = NSA_HEADS // NSA_KV_HEADS
HEAD_DIM = 128
NSA_WIDTH = NSA_HEADS * HEAD_DIM
CMP_STRIDE = 16
CMP_BLOCK = 2 * CMP_STRIDE
SEL_BLOCK = 64
SEG_PER_SEL = SEL_BLOCK // CMP_STRIDE
N_SELECT = 16
N_LOCAL_SEL = 2
WINDOW = 512
SCALE = HEAD_DIM ** -0.5
RET_HEADS = 4
RET_DK = 256
RET_DV = 256
RET_QK = RET_HEADS * RET_DK
RET_WIDTH = RET_HEADS * RET_DV
RET_CHUNK = 128
N_GROUPS = 4
EXPERTS_PER_GROUP = 8
N_EXPERTS = N_GROUPS * EXPERTS_PER_GROUP
TOP_K_IN_GROUP = 2
D_EXPERT = 1024
N_MOD = 6
EPS = 1e-6
NEG_INF = -1e30
KV_COLS = 3 * 2 * NSA_KV_HEADS * HEAD_DIM
OFF_NSA_KV = NSA_WIDTH
OFF_NSA_G = OFF_NSA_KV + KV_COLS
OFF_RET = OFF_NSA_G + 3 * NSA_HEADS
OFF_MERGE = OFF_RET + 2 * RET_QK + 2 * RET_WIDTH
IN_COLS = OFF_MERGE + 2 * D_MODEL

LANES = 128
SUBLANES = 8
KV_ROW = 2 * NSA_KV_HEADS
VMEM_LIMIT_BYTES = 56 * 1024 * 1024

ALIBI = tuple(tuple(2.0 ** -(g * NSA_GROUP + r + 1) for r in range(NSA_GROUP)) for g in range(NSA_KV_HEADS))


def _cparams(*sem):
    return pltpu.CompilerParams(dimension_semantics=sem, vmem_limit_bytes=VMEM_LIMIT_BYTES)


def _bf16(x):
    return x.astype(jnp.bfloat16)


def _dot(a, b):
    return jnp.dot(a, b, preferred_element_type=jnp.float32)


def _dot_nt(a, b):
    return lax.dot_general(a, b, (((1,), (1,)), ((), ())), preferred_element_type=jnp.float32)


def _dot_tn(a, b):
    return lax.dot_general(a, b, (((0,), (0,)), ((), ())), preferred_element_type=jnp.float32)


def _norm_mod_kernel(x_ref, g_ref, sc_ref, sh_ref, o_ref):
    x = x_ref[...]
    y = x * lax.rsqrt(jnp.mean(x * x, axis=-1, keepdims=True) + EPS)
    y = y * g_ref[...]
    h = y * (1.0 + sc_ref[...]) + sh_ref[...]
    tb, tt, d = x.shape
    o_ref[...] = h.reshape(tb * tt, d).astype(o_ref.dtype)


def _norm_mod(x, g, scale, shift, *, tb, tt):
    B, T, D = x.shape
    return pl.pallas_call(
        _norm_mod_kernel,
        out_shape=jax.ShapeDtypeStruct((B * T, D), jnp.bfloat16),
        grid=(B // tb, T // tt),
        in_specs=[
            pl.BlockSpec((tb, tt, D), lambda b, t: (b, t, 0)),
            pl.BlockSpec((1, 1, D), lambda b, t: (0, 0, 0)),
            pl.BlockSpec((tb, 1, D), lambda b, t: (b, 0, 0)),
            pl.BlockSpec((tb, 1, D), lambda b, t: (b, 0, 0)),
        ],
        out_specs=pl.BlockSpec((tb * tt, D), lambda b, t: (b * (T // tt) + t, 0)),
        compiler_params=_cparams("parallel", "parallel"),
        name="norm_mod",
    )(x, g.reshape(1, 1, D), scale, shift)


def _final_norm_kernel(x_ref, g_ref, o_ref):
    x = x_ref[...]
    y = x * lax.rsqrt(jnp.mean(x * x, axis=-1, keepdims=True) + EPS)
    o_ref[...] = y * g_ref[...]


def _final_norm(x2, g, *, tm):
    M, D = x2.shape
    return pl.pallas_call(
        _final_norm_kernel,
        out_shape=jax.ShapeDtypeStruct((M, D), jnp.float32),
        grid=(M // tm,),
        in_specs=[pl.BlockSpec((tm, D), lambda i: (i, 0)), pl.BlockSpec((1, D), lambda i: (0, 0))],
        out_specs=pl.BlockSpec((tm, D), lambda i: (i, 0)),
        compiler_params=_cparams("parallel"),
        name="final_norm",
    )(x2, g.reshape(1, D))


def _mm_kernel(x_ref, w_ref, *o_refs, act):
    acc = _dot(_bf16(x_ref[...]), _bf16(w_ref[0]))
    if act == "sigmoid":
        acc = jax.nn.sigmoid(acc)
    elif act == "silu":
        acc = acc * jax.nn.sigmoid(acc)
    for o_ref in o_refs:
        o_ref[...] = acc.astype(o_ref.dtype)


def _mm(x, w3, *, layer=0, col0=0, ncols=None, tm, tn, out_dtypes=(jnp.float32,), act=None, name="mm"):
    M, K = x.shape
    ncols = w3.shape[2] - col0 if ncols is None else ncols
    cb = col0 // tn
    outs = pl.pallas_call(
        functools.partial(_mm_kernel, act=act),
        out_shape=tuple(jax.ShapeDtypeStruct((M, ncols), dt) for dt in out_dtypes),
        grid=(M // tm, ncols // tn),
        in_specs=[pl.BlockSpec((tm, K), lambda i, j: (i, 0)),
                  pl.BlockSpec((1, K, tn), lambda i, j: (layer, 0, cb + j))],
        out_specs=tuple(pl.BlockSpec((tm, tn), lambda i, j: (i, j)) for _ in out_dtypes),
        compiler_params=_cparams("parallel", "parallel"),
        name=name,
    )(x, w3)
    return outs[0] if len(outs) == 1 else outs


def _merge_kernel(a0_ref, a1_ref, a2_ref, b_ref, g0_ref, g1_ref, wa_ref, wb_ref, o_ref):
    a = a0_ref[...] + a1_ref[...] + a2_ref[...]
    pa = _dot(_bf16(a), _bf16(wa_ref[0]))
    pb = _dot(_bf16(b_ref[...]), _bf16(wb_ref[0]))
    o_ref[...] = (jax.nn.sigmoid(g0_ref[...]) * pa + jax.nn.sigmoid(g1_ref[...]) * pb).astype(o_ref.dtype)


def _merge(o3, y_ret, zgm, w_a, w_b, *, layer, tm, tn):
    M, K = y_ret.shape
    N = w_a.shape[2]
    nj = N // tn
    row = pl.BlockSpec((tm, K), lambda i, j: (i, 0))
    return pl.pallas_call(
        _merge_kernel,
        out_shape=jax.ShapeDtypeStruct((M, N), jnp.bfloat16),
        grid=(M // tm, nj),
        in_specs=[row, row, row, row,
                  pl.BlockSpec((tm, tn), lambda i, j: (i, j)),
                  pl.BlockSpec((tm, tn), lambda i, j: (i, j + nj)),
                  pl.BlockSpec((1, K, tn), lambda i, j: (layer, 0, j)),
                  pl.BlockSpec((1, K, tn), lambda i, j: (layer, 0, j))],
        out_specs=pl.BlockSpec((tm, tn), lambda i, j: (i, j)),
        compiler_params=_cparams("parallel", "parallel"),
        name="merge_branches",
    )(*o3, y_ret, zgm, zgm, w_a, w_b)


def _resid_mm_kernel(t_ref, w_ref, x_ref, gt_ref, o_ref):
    acc = _dot(t_ref[...], _bf16(w_ref[0]))
    tb, tt, tn = x_ref.shape
    o_ref[...] = x_ref[...] + gt_ref[...] * acc.reshape(tb, tt, tn)


def _resid_mm(t, w3, x, gate, *, layer, tb, tt, tn):
    B, T, N = x.shape
    K = t.shape[1]
    nt = T // tt
    return pl.pallas_call(
        _resid_mm_kernel,
        out_shape=jax.ShapeDtypeStruct((B, T, N), jnp.float32),
        grid=(B // tb, nt, N // tn),
        in_specs=[
            pl.BlockSpec((tb * tt, K), lambda b, t_, j: (b * nt + t_, 0)),
            pl.BlockSpec((1, K, tn), lambda b, t_, j: (layer, 0, j)),
            pl.BlockSpec((tb, tt, tn), lambda b, t_, j: (b, t_, j)),
            pl.BlockSpec((tb, 1, tn), lambda b, t_, j: (b, 0, j)),
        ],
        out_specs=pl.BlockSpec((tb, tt, tn), lambda b, t_, j: (b, t_, j)),
        compiler_params=_cparams("parallel", "parallel", "parallel"),
        name="resid_out_proj",
    )(t, w3, x, gate)


def _sel_matrix(nb, ns, nb_pad, ns_pad):
    i = np.arange(nb_pad)[:, None]
    j = np.arange(ns_pad)[None, :]
    lo, hi = SEG_PER_SEL * j, SEG_PER_SEL * j + SEG_PER_SEL - 1
    m = ((i >= lo) & (i <= hi)).astype(np.float32) + ((i + 1 >= lo) & (i + 1 <= hi)).astype(np.float32)
    m = m * ((i < nb) & (j < ns))
    return jnp.asarray(m, jnp.bfloat16)


def _block_importance(imp, msel):
    hi = _bf16(imp)
    r1 = imp - hi.astype(jnp.float32)
    mid = _bf16(r1)
    lo = _bf16(r1 - mid.astype(jnp.float32))
    return _dot(hi, msel) + _dot(mid, msel) + _dot(lo, msel)


def _pick_blocks(sel, qpos, ns):
    nsp = sel.shape[1]
    j = lax.broadcasted_iota(jnp.int32, (1, nsp), 1)
    cur = qpos // SEL_BLOCK
    valid = (j <= cur) & (j < ns)
    forced = (j == 0) | (valid & (j > cur - N_LOCAL_SEL))
    score = jnp.where(forced, 1e9, jnp.where(valid, sel, -1e9))
    score = jnp.where(j < ns, score, -3e38)
    lane = j.astype(jnp.float32)
    chosen = jnp.zeros(sel.shape, jnp.float32)
    for _ in range(min(N_SELECT, ns)):
        m = jnp.max(score, axis=-1, keepdims=True)
        first = jnp.min(jnp.where(score == m, lane, float(nsp)), axis=-1, keepdims=True)
        hit = lane == first
        chosen = jnp.where(hit, 1.0, chosen)
        score = jnp.where(hit, -3e38, score)
    return jnp.where(valid, chosen, 0.0)


def _softmax_rows(s, mask):
    s = jnp.where(mask, s, NEG_INF)
    e = jnp.exp(s - jnp.max(s, axis=-1, keepdims=True))
    return jnp.where(mask, e * (1.0 / jnp.sum(e, axis=-1, keepdims=True)), 0.0)


def _compress_kernel(kv_ref, phic_ref, a_ref, b_ref, *, n):
    x = jnp.concatenate([kv_ref[pl.ds(l, n, stride=CMP_STRIDE), :] for l in range(CMP_STRIDE)], axis=1)
    y = _dot(_bf16(x), phic_ref[0])
    a_ref[0] = y[:, :HEAD_DIM]
    b_ref[0] = y[:, HEAD_DIM:]


def _compress_prompt(kv, phic, *, n):
    T = kv.shape[0]
    nseg = T // CMP_STRIDE
    out = jax.ShapeDtypeStruct((KV_ROW, nseg, HEAD_DIM), jnp.float32)
    return pl.pallas_call(
        functools.partial(_compress_kernel, n=n),
        out_shape=(out, out),
        grid=(KV_ROW, nseg // n),
        in_specs=[pl.BlockSpec((n * CMP_STRIDE, HEAD_DIM), lambda j, i: (i, j)),
                  pl.BlockSpec((1, CMP_STRIDE * HEAD_DIM, 2 * HEAD_DIM), lambda j, i: (j // NSA_KV_HEADS, 0, 0))],
        out_specs=(pl.BlockSpec((1, n, HEAD_DIM), lambda j, i: (j, i, 0)),
                   pl.BlockSpec((1, n, HEAD_DIM), lambda j, i: (j, i, 0))),
        compiler_params=_cparams("parallel", "parallel"),
        name="nsa_compress",
    )(kv, phic)


def _gate_col(gate_ref, g, branch, r):
    c0 = branch * NSA_HEADS + r
    c1 = c0 + NSA_GROUP
    return jnp.where(g == 0, gate_ref[:, c0:c0 + 1], gate_ref[:, c1:c1 + 1])


def _cmp_sel_kernel(q_ref, kc_ref, vc_ref, msel_ref, gate_ref, o_ref, sel_ref, *, tq, nb, ns):
    g = pl.program_id(0)
    qi = pl.program_id(1)
    qpos = qi * tq + lax.broadcasted_iota(jnp.int32, (tq, 1), 0)
    endpos = lax.broadcasted_iota(jnp.int32, (1, nb), 1) * CMP_STRIDE + (CMP_BLOCK - 1)
    dist = qpos - endpos
    mask = dist >= 0
    distf = dist.astype(jnp.float32)
    kc = kc_ref[0]
    vc = vc_ref[0]
    imp = jnp.zeros((tq, nb), jnp.float32)
    for r in range(NSA_GROUP):
        slope = jnp.where(g == 0, ALIBI[0][r], ALIBI[1][r])
        s = _dot_nt(q_ref[:, r * HEAD_DIM:(r + 1) * HEAD_DIM], kc) * SCALE - slope * distf
        p = _softmax_rows(s, mask)
        o_ref[:, r * HEAD_DIM:(r + 1) * HEAD_DIM] = _gate_col(gate_ref, g, 0, r) * _dot(_bf16(p), vc)
        imp = imp + p
    sel = _block_importance(imp, msel_ref[...])
    sel_ref[0] = _pick_blocks(sel, qpos, ns).astype(sel_ref.dtype)


def _cmp_sel(qb, kc, vc, gates, *, tq):
    T = qb.shape[0]
    nb = kc.shape[1]
    ns = T // SEL_BLOCK
    gw = NSA_GROUP * HEAD_DIM
    return pl.pallas_call(
        functools.partial(_cmp_sel_kernel, tq=tq, nb=nb, ns=ns),
        out_shape=(jax.ShapeDtypeStruct((T, NSA_WIDTH), jnp.float32),
                   jax.ShapeDtypeStruct((NSA_KV_HEADS, T, ns), jnp.bfloat16)),
        grid=(NSA_KV_HEADS, T // tq),
        in_specs=[
            pl.BlockSpec((tq, gw), lambda g, i: (i, g)),
            pl.BlockSpec((1, nb, HEAD_DIM), lambda g, i: (g, 0, 0)),
            pl.BlockSpec((1, nb, HEAD_DIM), lambda g, i: (g, 0, 0)),
            pl.BlockSpec((nb, ns), lambda g, i: (0, 0)),
            pl.BlockSpec((tq, LANES), lambda g, i: (i, 0)),
        ],
        out_specs=(pl.BlockSpec((tq, gw), lambda g, i: (i, g)),
                   pl.BlockSpec((1, tq, ns), lambda g, i: (g, i, 0))),
        compiler_params=_cparams("parallel", "parallel"),
        name="nsa_cmp_select",
    )(qb, kc, vc, _sel_matrix(nb, ns, nb, ns), gates)


def _flash_kernel(*refs, tq, tk, nk, mode, branch):
    if mode == "sel":
        q_ref, k_ref, v_ref, gate_ref, sel_ref, o_ref, m_scr, l_scr, acc_scr = refs
    else:
        q_ref, k_ref, v_ref, gate_ref, o_ref, m_scr, l_scr, acc_scr = refs
    g = pl.program_id(0)
    qi = pl.program_id(1)
    kj = pl.program_id(2)
    if mode == "sel":
        kt = kj
        valid = kj * tk <= qi * tq + (tq - 1)
    else:
        kt = qi * (tq // tk) - WINDOW // tk + kj
        valid = kt >= 0

    @pl.when(kj == 0)
    def _():
        m_scr[...] = jnp.full(m_scr.shape, NEG_INF, jnp.float32)
        l_scr[...] = jnp.zeros(l_scr.shape, jnp.float32)
        acc_scr[...] = jnp.zeros(acc_scr.shape, jnp.float32)

    @pl.when(valid)
    def _():
        k = k_ref[...]
        v = v_ref[...]
        kpos = kt * tk + lax.broadcasted_iota(jnp.int32, (1, tk), 1)
        dist = (qi * tq + lax.broadcasted_iota(jnp.int32, (tq, 1), 0)) - kpos
        if mode == "sel":
            blk = lax.broadcasted_iota(jnp.int32, (sel_ref.shape[2], 1), 0)
            expand = _bf16(kpos // SEL_BLOCK == blk)
            msk = (_dot(sel_ref[0], expand) > 0.5) & (dist >= 0)
        else:
            msk = (dist >= 0) & (dist < WINDOW)
        distf = dist.astype(jnp.float32)
        for r in range(NSA_GROUP):
            slope = jnp.where(g == 0, ALIBI[0][r], ALIBI[1][r])
            s = _dot_nt(q_ref[:, r * HEAD_DIM:(r + 1) * HEAD_DIM], k)
            s = jnp.where(msk, s * SCALE - slope * distf, NEG_INF)
            m_prev = m_scr[r]
            m_new = jnp.maximum(m_prev, jnp.max(s, axis=-1, keepdims=True))
            alpha = jnp.exp(m_prev - m_new)
            p = jnp.where(msk, jnp.exp(s - m_new), 0.0)
            l_scr[r] = alpha * l_scr[r] + jnp.sum(p, axis=-1, keepdims=True)
            acc_scr[r] = alpha * acc_scr[r] + _dot(_bf16(p), v)
            m_scr[r] = m_new

    @pl.when(kj == nk - 1)
    def _():
        for r in range(NSA_GROUP):
            o_ref[:, r * HEAD_DIM:(r + 1) * HEAD_DIM] = (
                acc_scr[r] * (_gate_col(gate_ref, g, branch, r) / l_scr[r]))


def _flash(qb, kvb, gates, selm, *, branch, mode, tq, tk):
    T = qb.shape[0]
    gw = NSA_GROUP * HEAD_DIM
    kcol = (branch * 2 + 0) * NSA_KV_HEADS
    vcol = (branch * 2 + 1) * NSA_KV_HEADS
    if mode == "sel":
        nk = T // tk

        def ktile(i, j):
            return jnp.minimum(j, (i * tq + tq - 1) // tk)
    else:
        nk = (WINDOW + tq) // tk

        def ktile(i, j):
            return jnp.maximum(i * (tq // tk) - WINDOW // tk + j, 0)

    in_specs = [
        pl.BlockSpec((tq, gw), lambda g, i, j: (i, g)),
        pl.BlockSpec((tk, HEAD_DIM), lambda g, i, j: (ktile(i, j), kcol + g)),
        pl.BlockSpec((tk, HEAD_DIM), lambda g, i, j: (ktile(i, j), vcol + g)),
        pl.BlockSpec((tq, LANES), lambda g, i, j: (i, 0)),
    ]
    args = [qb, kvb, kvb, gates]
    if mode == "sel":
        ns = selm.shape[2]
        in_specs.append(pl.BlockSpec((1, tq, ns), lambda g, i, j: (g, i, 0)))
        args.append(selm)
    return pl.pallas_call(
        functools.partial(_flash_kernel, tq=tq, tk=tk, nk=nk, mode=mode, branch=branch),
        out_shape=jax.ShapeDtypeStruct((T, NSA_WIDTH), jnp.float32),
        grid=(NSA_KV_HEADS, T // tq, nk),
        in_specs=in_specs,
        out_specs=pl.BlockSpec((tq, gw), lambda g, i, j: (i, g)),
        scratch_shapes=[pltpu.VMEM((NSA_GROUP, tq, 1), jnp.float32),
                        pltpu.VMEM((NSA_GROUP, tq, 1), jnp.float32),
                        pltpu.VMEM((NSA_GROUP, tq, HEAD_DIM), jnp.float32)],
        compiler_params=_cparams("parallel", "parallel", "arbitrary"),
        name="nsa_" + mode + "_attn",
    )(*args)


def _nsa_sample_kernel(pt_ref, q_ref, kvn_ref, gate_ref, *rest, n_pages, tq, nb, ns, lk, lw):
    cmp_refs = rest[:n_pages]
    slc_refs = rest[n_pages:2 * n_pages]
    win_ref, phic_ref, cst_ref, msel_ref, o_ref, wout_ref, x_scr, k_scr, v_scr = rest[2 * n_pages:]
    del pt_ref
    past = n_pages * PAGE_SIZE
    wb = win_ref.shape[2] // KV_ROW
    rows = NSA_GROUP * tq
    row = lax.broadcasted_iota(jnp.int32, (rows, 1), 0)
    qpos = past + (row % tq)
    rsel = row // tq
    qf = q_ref[0].astype(jnp.float32)
    nbp = x_scr.shape[0]
    zero_tail = jnp.zeros((LANES - tq, HEAD_DIM), jnp.float32)

    def kv_new(branch, c, g):
        col = ((branch * 2 + c) * NSA_KV_HEADS + g) * HEAD_DIM
        return kvn_ref[:, col:col + HEAD_DIM]

    def gated(o, g, branch):
        for r in range(NSA_GROUP):
            c = branch * NSA_HEADS + g * NSA_GROUP + r
            col = (g * NSA_GROUP + r) * HEAD_DIM
            val = gate_ref[:, c:c + 1] * o[r * tq:(r + 1) * tq]
            if branch == 0:
                o_ref[:, col:col + HEAD_DIM] = val
            else:
                o_ref[:, col:col + HEAD_DIM] += val

    ckv = [[None, None], [None, None]]
    for c in range(2):
        for g in range(NSA_KV_HEADS):
            j = c * NSA_KV_HEADS + g
            for p in range(n_pages):
                for l in range(CMP_STRIDE):
                    x_scr[p * SUBLANES:(p + 1) * SUBLANES, l * HEAD_DIM:(l + 1) * HEAD_DIM] = (
                        cmp_refs[p][0, 0, pl.ds(j + KV_ROW * l, SUBLANES, stride=KV_ROW * CMP_STRIDE), :])
            base = n_pages * SUBLANES
            x_scr[base:nbp, :] = jnp.zeros((nbp - base, x_scr.shape[1]), jnp.float32)
            new = kv_new(0, c, g)
            for l in range(tq):
                x_scr[base:base + 1, l * HEAD_DIM:(l + 1) * HEAD_DIM] = new[l:l + 1]
            y = _dot(_bf16(x_scr[...]), phic_ref[c])
            second = jnp.concatenate([y[1:, HEAD_DIM:], jnp.zeros((1, HEAD_DIM), jnp.float32)], axis=0)
            ckv[c][g] = _bf16(y[:, :HEAD_DIM] + second + cst_ref[c])

    endpos = lax.broadcasted_iota(jnp.int32, (1, nbp), 1) * CMP_STRIDE + (CMP_BLOCK - 1)
    dist_c = qpos - endpos
    mask_c = (dist_c >= 0) & (lax.broadcasted_iota(jnp.int32, (1, nbp), 1) < nb)
    kpos_s = lax.broadcasted_iota(jnp.int32, (1, lk), 1)
    dist_s = qpos - kpos_s
    blk = lax.broadcasted_iota(jnp.int32, (msel_ref.shape[1], 1), 0)
    expand = _bf16(kpos_s // SEL_BLOCK == blk)
    kpos_w = past - wb + lax.broadcasted_iota(jnp.int32, (1, lw), 1)
    dist_w = qpos - kpos_w
    mask_w = (dist_w >= 0) & (dist_w < WINDOW)

    for g in range(NSA_KV_HEADS):
        slope = jnp.zeros((rows, 1), jnp.float32)
        for r in range(NSA_GROUP):
            slope = jnp.where(rsel == r, ALIBI[g][r], slope)
        qg = _bf16(jnp.concatenate(
            [qf[:, (g * NSA_GROUP + r) * HEAD_DIM:(g * NSA_GROUP + r + 1) * HEAD_DIM] for r in range(NSA_GROUP)],
            axis=0))
        s = _dot_nt(qg, ckv[0][g]) * SCALE - slope * dist_c.astype(jnp.float32)
        p = _softmax_rows(s, mask_c)
        gated(_dot(_bf16(p), ckv[1][g]), g, 0)
        imp = p[0:tq]
        for r in range(1, NSA_GROUP):
            imp = imp + p[r * tq:(r + 1) * tq]
        imp = jnp.concatenate([imp, jnp.zeros((2 * SUBLANES - tq, nbp), jnp.float32)], axis=0)
        sel = _block_importance(imp, msel_ref[...])
        picked = _pick_blocks(sel, past + lax.broadcasted_iota(jnp.int32, (2 * SUBLANES, 1), 0), ns)
        for p_i in range(n_pages):
            k_scr[p_i * PAGE_SIZE:(p_i + 1) * PAGE_SIZE, :] = _bf16(
                slc_refs[p_i][0, 0, pl.ds(g, PAGE_SIZE, stride=KV_ROW), :])
            v_scr[p_i * PAGE_SIZE:(p_i + 1) * PAGE_SIZE, :] = _bf16(
                slc_refs[p_i][0, 0, pl.ds(NSA_KV_HEADS + g, PAGE_SIZE, stride=KV_ROW), :])
        k_scr[past:lk, :] = _bf16(jnp.concatenate([kv_new(1, 0, g), zero_tail], axis=0))
        v_scr[past:lk, :] = _bf16(jnp.concatenate([kv_new(1, 1, g), zero_tail], axis=0))
        keep = _dot(_bf16(picked), expand)[0:tq]
        keep = jnp.concatenate([keep] * NSA_GROUP, axis=0)
        s = _dot_nt(qg, k_scr[...]) * SCALE - slope * dist_s.astype(jnp.float32)
        p = _softmax_rows(s, (keep > 0.5) & (dist_s >= 0))
        gated(_dot(_bf16(p), v_scr[...]), g, 1)
        kw = jnp.concatenate([win_ref[0, 0, pl.ds(g, wb, stride=KV_ROW), :], kv_new(2, 0, g), zero_tail], axis=0)
        vw = jnp.concatenate([win_ref[0, 0, pl.ds(NSA_KV_HEADS + g, wb, stride=KV_ROW), :], kv_new(2, 1, g),
                              zero_tail], axis=0)
        s = _dot_nt(qg, _bf16(kw)) * SCALE - slope * dist_w.astype(jnp.float32)
        p = _softmax_rows(s, mask_w)
        gated(_dot(_bf16(p), _bf16(vw)), g, 2)

    keep_rows = (wb - tq) * KV_ROW
    wout_ref[0, 0, 0:keep_rows, :] = win_ref[0, 0, tq * KV_ROW:wb * KV_ROW, :]
    for t in range(tq):
        for j in range(KV_ROW):
            col = (2 * KV_ROW + j) * HEAD_DIM
            wout_ref[0, 0, keep_rows + t * KV_ROW + j:keep_rows + t * KV_ROW + j + 1, :] = (
                kvn_ref[t:t + 1, col:col + HEAD_DIM])


def _nsa_sample(q3, kvn, gates, cache_cmp, cache_slc, state_win, win_out, page_table, phic, cst, *, layer):
    Bd, Tq, _ = q3.shape
    n_pages = page_table.shape[1]
    past = n_pages * PAGE_SIZE
    total = past + Tq
    l_pad = -(-total // SEL_BLOCK) * SEL_BLOCK
    nb = l_pad // CMP_STRIDE
    ns = l_pad // SEL_BLOCK
    nbp = -(-nb // SUBLANES) * SUBLANES
    lk = past + LANES
    wrows = state_win.shape[2]
    lw = wrows // KV_ROW + LANES
    page_rows = PAGE_SIZE * KV_ROW

    def page_spec(p):
        return pl.BlockSpec((1, 1, page_rows, HEAD_DIM), lambda b, pt: (layer, pt[b * n_pages + p], 0, 0))

    grid_spec = pltpu.PrefetchScalarGridSpec(
        num_scalar_prefetch=1,
        grid=(Bd,),
        in_specs=[pl.BlockSpec((1, Tq, NSA_WIDTH), lambda b, pt: (b, 0, 0)),
                  pl.BlockSpec((Tq, KV_COLS), lambda b, pt: (b, 0)),
                  pl.BlockSpec((Tq, LANES), lambda b, pt: (b, 0))]
        + [page_spec(p) for p in range(n_pages)] + [page_spec(p) for p in range(n_pages)]
        + [pl.BlockSpec((1, 1, wrows, HEAD_DIM), lambda b, pt: (layer, b, 0, 0)),
           pl.BlockSpec(phic.shape, lambda b, pt: (0, 0, 0)),
           pl.BlockSpec(cst.shape, lambda b, pt: (0, 0, 0)),
           pl.BlockSpec((nbp, LANES), lambda b, pt: (0, 0)),
           pl.BlockSpec(memory_space=pl.ANY)],
        out_specs=(pl.BlockSpec((Tq, NSA_WIDTH), lambda b, pt: (b, 0)),
                   pl.BlockSpec((1, 1, wrows, HEAD_DIM), lambda b, pt: (layer, b, 0, 0))),
        scratch_shapes=[pltpu.VMEM((nbp, CMP_STRIDE * HEAD_DIM), jnp.float32),
                        pltpu.VMEM((lk, HEAD_DIM), jnp.bfloat16),
                        pltpu.VMEM((lk, HEAD_DIM), jnp.bfloat16)],
    )
    n_in = 3 + 2 * n_pages + 4

    def body(pt_ref, *refs):
        ins = refs[:n_in]
        outs_scr = refs[n_in + 1:]
        _nsa_sample_kernel(pt_ref, *ins, *outs_scr, n_pages=n_pages, tq=Tq, nb=nb, ns=ns, lk=lk, lw=lw)

    return pl.pallas_call(
        body,
        out_shape=(jax.ShapeDtypeStruct((Bd * Tq, NSA_WIDTH), jnp.float32),
                   jax.ShapeDtypeStruct(win_out.shape, win_out.dtype)),
        grid_spec=grid_spec,
        input_output_aliases={n_in + 1: 1},
        compiler_params=_cparams("arbitrary"),
        name="nsa_sample",
    )(page_table.reshape(-1), q3, kvn, gates, *([cache_cmp] * n_pages), *([cache_slc] * n_pages),
      state_win, phic, cst, _sel_matrix(nb, ns, nbp, LANES), win_out)


def _decay_tables(chunk):
    log_gamma = jnp.log1p(-jnp.exp2(-5.0 - jnp.arange(RET_HEADS, dtype=jnp.float32)))
    i = jnp.arange(chunk, dtype=jnp.float32)
    diff = i[:, None] - i[None, :]
    d_in = jnp.where(diff >= 0, jnp.exp(jnp.maximum(diff, 0.0)[None] * log_gamma[:, None, None]), 0.0)
    d_q = jnp.exp((i + 1.0)[None, :] * log_gamma[:, None])[:, :, None]
    d_k = jnp.exp((chunk - 1.0 - i)[None, :] * log_gamma[:, None])[:, :, None]
    d_c = jnp.exp(chunk * log_gamma)[:, None, None]
    return d_in, d_q, d_k, d_c


def _ret_head(q, k, v, s_prev, d_in, d_q, d_k, d_c):
    kf = k * RET_DK ** -0.5
    att = _dot_nt(_bf16(q), _bf16(kf)) * d_in
    o = _dot(_bf16(att), _bf16(v)) + _dot(_bf16(q), _bf16(s_prev)) * d_q
    s_new = s_prev * d_c + _dot_tn(_bf16(kf * d_k), _bf16(v))
    return o, s_new


def _group_norm_gate(o, gn, rg):
    mu = jnp.mean(o, axis=-1, keepdims=True)
    var = jnp.mean(jnp.square(o - mu), axis=-1, keepdims=True)
    return rg * (((o - mu) * lax.rsqrt(var + EPS)) * gn)


def _ret_prompt_kernel(q_ref, k_ref, v_ref, rg_ref, gn_ref, din_ref, dq_ref, dk_ref, dc_ref,
                       y_ref, s_ref, st_scr, *, nc):
    c = pl.program_id(0)

    @pl.when(c == 0)
    def _():
        st_scr[...] = jnp.zeros(st_scr.shape, jnp.float32)

    for h in range(RET_HEADS):
        sl = slice(h * RET_DK, (h + 1) * RET_DK)
        o, s_new = _ret_head(q_ref[:, sl].astype(jnp.float32), k_ref[:, sl].astype(jnp.float32),
                             v_ref[:, sl].astype(jnp.float32), st_scr[h],
                             din_ref[h], dq_ref[h], dk_ref[h], dc_ref[h])
        st_scr[h] = s_new
        y_ref[:, sl] = _group_norm_gate(o, gn_ref[:, sl], rg_ref[:, sl]).astype(y_ref.dtype)

    @pl.when(c == nc - 1)
    def _():
        s_ref[...] = st_scr[...]


def _ret_prompt(rqkv, rg, gn, *, chunk):
    T = rqkv.shape[0]
    nc = T // chunk
    tabs = _decay_tables(chunk)
    full = lambda a: pl.BlockSpec(a.shape, lambda c: (0,) * a.ndim)
    return pl.pallas_call(
        functools.partial(_ret_prompt_kernel, nc=nc),
        out_shape=(jax.ShapeDtypeStruct((T, RET_WIDTH), jnp.bfloat16),
                   jax.ShapeDtypeStruct((RET_HEADS, RET_DK, RET_DV), jnp.float32)),
        grid=(nc,),
        in_specs=[pl.BlockSpec((chunk, RET_QK), lambda c: (c, 0)),
                  pl.BlockSpec((chunk, RET_QK), lambda c: (c, 1)),
                  pl.BlockSpec((chunk, RET_WIDTH), lambda c: (c, 2)),
                  pl.BlockSpec((chunk, RET_WIDTH), lambda c: (c, 0)),
                  full(gn)] + [full(t) for t in tabs],
        out_specs=(pl.BlockSpec((chunk, RET_WIDTH), lambda c: (c, 0)),
                   pl.BlockSpec((RET_HEADS, RET_DK, RET_DV), lambda c: (0, 0, 0))),
        scratch_shapes=[pltpu.VMEM((RET_HEADS, RET_DK, RET_DV), jnp.float32)],
        compiler_params=_cparams("arbitrary"),
        name="retention_prompt",
    )(rqkv, rqkv, rqkv, rg, gn, *tabs)


def _ret_sample_kernel(qkv_ref, rg_ref, gn_ref, din_ref, dq_ref, dk_ref, dc_ref, s0_ref, _, y_ref, s_ref, *, tq):
    x = qkv_ref[0].astype(jnp.float32)
    x = jnp.concatenate([x, jnp.zeros((2 * SUBLANES - tq, x.shape[1]), jnp.float32)], axis=0)
    for h in range(RET_HEADS):
        sl = slice(h * RET_DK, (h + 1) * RET_DK)
        o, s_new = _ret_head(x[:, h * RET_DK:(h + 1) * RET_DK],
                             x[:, RET_QK + h * RET_DK:RET_QK + (h + 1) * RET_DK],
                             x[:, 2 * RET_QK + h * RET_DV:2 * RET_QK + (h + 1) * RET_DV],
                             s0_ref[0, 0, h], din_ref[h], dq_ref[h], dk_ref[h], dc_ref[h])
        s_ref[0, 0, h] = s_new
        y_ref[:, sl] = _group_norm_gate(o[0:tq], gn_ref[:, sl], rg_ref[:, sl])


def _ret_sample(rqkv3, rg, gn, state, state_out, *, layer):
    Bd, Tq, _ = rqkv3.shape
    pad = 2 * SUBLANES
    d_in, d_q, d_k, d_c = _decay_tables(Tq)
    d_in = jnp.pad(d_in, ((0, 0), (0, pad - Tq), (0, pad - Tq)))
    d_q = jnp.pad(d_q, ((0, 0), (0, pad - Tq), (0, 0)))
    d_k = jnp.pad(d_k, ((0, 0), (0, pad - Tq), (0, 0)))
    tabs = (d_in, d_q, d_k, d_c)
    full = lambda a: pl.BlockSpec(a.shape, lambda b: (0,) * a.ndim)
    st_spec = pl.BlockSpec((1, 1, RET_HEADS, RET_DK, RET_DV), lambda b: (layer, b, 0, 0, 0))
    return pl.pallas_call(
        functools.partial(_ret_sample_kernel, tq=Tq),
        out_shape=(jax.ShapeDtypeStruct((Bd * Tq, RET_WIDTH), jnp.float32),
                   jax.ShapeDtypeStruct(state_out.shape, state_out.dtype)),
        grid=(Bd,),
        in_specs=[pl.BlockSpec((1, Tq, 3 * RET_QK), lambda b: (b, 0, 0)),
                  pl.BlockSpec((Tq, RET_WIDTH), lambda b: (b, 0)),
                  full(gn)] + [full(t) for t in tabs] + [st_spec, pl.BlockSpec(memory_space=pl.ANY)],
        out_specs=(pl.BlockSpec((Tq, RET_WIDTH), lambda b: (b, 0)), st_spec),
        input_output_aliases={8: 1},
        compiler_params=_cparams("arbitrary"),
        name="retention_sample",
    )(rqkv3, rg, gn, *tabs, state, state_out)


def _moe_kernel(tile_ref, exp_ref, lo_ref, hi_ref, first_ref, valid_ref,
                xs_ref, ws_ref, wg_ref, wu_ref, wd_ref, o_ref, *, tm):
    k = pl.program_id(0)
    f = pl.program_id(1)

    @pl.when((first_ref[k] == 1) & (f == 0))
    def _():
        o_ref[...] = jnp.zeros(o_ref.shape, jnp.float32)

    @pl.when(valid_ref[k] == 1)
    def _():
        x = xs_ref[...]
        a = _dot(x, _bf16(wg_ref[0]))
        u = _dot(x, _bf16(wu_ref[0]))
        rows = tile_ref[k] * tm + lax.broadcasted_iota(jnp.int32, (tm, 1), 0)
        mine = (rows >= lo_ref[k]) & (rows < hi_ref[k])
        h = jnp.where(mine, a * jax.nn.sigmoid(a) * u * ws_ref[...], 0.0)
        o_ref[...] += _dot(_bf16(h), _bf16(wd_ref[0]))


def _moe_experts(xs, ws, meta, w_gate, w_up, w_down, *, layer, tm, tf):
    R, D = xs.shape
    F = w_gate.shape[3]
    nf = F // tf
    n_items = meta[0].shape[0]

    def fidx(k, f, valid):
        return jnp.where(valid[k] == 1, f, nf - 1)

    grid_spec = pltpu.PrefetchScalarGridSpec(
        num_scalar_prefetch=6,
        grid=(n_items, nf),
        in_specs=[
            pl.BlockSpec((tm, D), lambda k, f, t, e, lo, hi, fi, va: (t[k], 0)),
            pl.BlockSpec((tm, 1), lambda k, f, t, e, lo, hi, fi, va: (t[k], 0)),
            pl.BlockSpec((None, 1, D, tf), lambda k, f, t, e, lo, hi, fi, va: (layer, e[k], 0, fidx(k, f, va))),
            pl.BlockSpec((None, 1, D, tf), lambda k, f, t, e, lo, hi, fi, va: (layer, e[k], 0, fidx(k, f, va))),
            pl.BlockSpec((None, 1, tf, D), lambda k, f, t, e, lo, hi, fi, va: (layer, e[k], fidx(k, f, va), 0)),
        ],
        out_specs=pl.BlockSpec((tm, D), lambda k, f, t, e, lo, hi, fi, va: (t[k], 0)),
    )
    return pl.pallas_call(
        functools.partial(_moe_kernel, tm=tm),
        out_shape=jax.ShapeDtypeStruct((R, D), jnp.float32),
        grid_spec=grid_spec,
        compiler_params=_cparams("arbitrary", "arbitrary"),
        name="moe_experts",
    )(*meta, xs, ws, w_gate, w_up, w_down)


def _moe_meta(sizes, *, tm, n_tiles):
    n_items = n_tiles + N_EXPERTS - 1
    ends = jnp.cumsum(sizes)
    starts = ends - sizes
    first_tile = starts // tm
    last_tile = jnp.maximum(ends - 1, 0) // tm
    cnt = jnp.where(sizes > 0, last_tile - first_tile + 1, 0)
    cum = jnp.cumsum(cnt)
    total = cum[-1]
    k = jnp.arange(n_items, dtype=jnp.int32)
    kk = jnp.minimum(k, total - 1)
    e = jnp.sum((cum[None, :] <= kk[:, None]).astype(jnp.int32), axis=1)
    tile = first_tile[e] + (kk - (cum[e] - cnt[e]))
    valid = (k < total).astype(jnp.int32)
    first = jnp.concatenate([jnp.ones((1,), jnp.int32), (tile[1:] != tile[:-1]).astype(jnp.int32)]) * valid
    return (tile.astype(jnp.int32), e.astype(jnp.int32), starts[e].astype(jnp.int32), ends[e].astype(jnp.int32),
            first, valid)


def _hier_moe(hb, w_route, rg_b, re_b, w_gate, w_up, w_down, *, layer, tm=512, tf=256):
    N, D = hb.shape
    n_route = N_GROUPS + N_EXPERTS
    logits = _mm(hb, w_route, layer=layer, tm=1024, tn=LANES, name="moe_router")
    g_logits = logits[:, :N_GROUPS] + rg_b
    e_logits = (logits[:, N_GROUPS:n_route] + re_b).reshape(N, N_GROUPS, EXPERTS_PER_GROUP)
    g_sel = jnp.argmax(g_logits, axis=-1)
    g_w = jnp.take_along_axis(jax.nn.softmax(g_logits, axis=-1), g_sel[:, None], axis=-1)
    e_in = jnp.take_along_axis(e_logits, g_sel[:, None, None], axis=1)[:, 0]
    top_v, top_i = lax.top_k(e_in, TOP_K_IN_GROUP)
    w = jax.nn.softmax(top_v, axis=-1) * g_w
    flat_e = (g_sel[:, None] * EXPERTS_PER_GROUP + top_i).reshape(-1).astype(jnp.int32)
    R = N * TOP_K_IN_GROUP
    onehot = (flat_e[:, None] == jnp.arange(N_EXPERTS, dtype=jnp.int32)[None, :]).astype(jnp.int32)
    csum = jnp.cumsum(onehot, axis=0)
    sizes = csum[-1]
    rank = jnp.sum(csum * onehot, axis=1) - 1
    starts = jnp.cumsum(sizes) - sizes
    dest = starts[flat_e] + rank
    tok = jnp.zeros((R,), jnp.int32).at[dest].set(jnp.arange(R, dtype=jnp.int32) // TOP_K_IN_GROUP)
    ws = jnp.zeros((R,), jnp.float32).at[dest].set(w.reshape(-1))[:, None]
    xs = hb[tok]
    meta = _moe_meta(sizes, tm=tm, n_tiles=R // tm)
    ys = _moe_experts(xs, ws, meta, w_gate, w_up, w_down, layer=layer, tm=tm, tf=tf)
    return ys[dest].reshape(N, TOP_K_IN_GROUP, D).sum(axis=1)


def _project(hb, w_in, w_gate_cols, w_tail, *, layer, tm):
    qb = _mm(hb, w_in, layer=layer, col0=0, ncols=OFF_NSA_KV, tm=tm, tn=512, out_dtypes=(jnp.bfloat16,),
             name="proj_q")
    kv, kvb = _mm(hb, w_in, layer=layer, col0=OFF_NSA_KV, ncols=KV_COLS, tm=tm, tn=512,
                  out_dtypes=(jnp.float32, jnp.bfloat16), name="proj_kv")
    gates = _mm(hb, w_gate_cols, layer=layer, tm=tm, tn=LANES, act="sigmoid", name="proj_gate")
    rqkv = _mm(hb, w_tail, layer=layer, col0=0, ncols=3 * RET_QK, tm=tm, tn=512, out_dtypes=(jnp.bfloat16,),
               name="proj_ret")
    rg = _mm(hb, w_tail, layer=layer, col0=3 * RET_QK, ncols=RET_WIDTH, tm=tm, tn=512, act="silu",
             name="proj_ret_gate")
    zgm = _mm(hb, w_tail, layer=layer, col0=3 * RET_QK + RET_WIDTH, ncols=2 * D_MODEL, tm=tm, tn=512,
              name="proj_merge")
    return qb, kv, kvb, gates, rqkv, rg, zgm


def kernel(x_prompt, x_sample, c_prompt, c_sample, cache_cmp_kv, cache_slc_kv, state_win_kv, state_ret, page_table, norm_mix_g, norm_ffn_g, norm_final_g, w_ada, b_ada, w_in, cmp_pe, cmp_phi, ret_gn_g, w_branch_nsa, w_branch_ret, w_out, router_group_w, router_group_b, router_expert_w, router_expert_b, expert_w_gate, expert_w_up, expert_w_down):
    xp, xs = x_prompt, x_sample
    Bp, Tp, D = xp.shape
    Bs, Ts, _ = xs.shape
    Np, Ns = Bp * Tp, Bs * Ts
    win_p = min(WINDOW, Tp)
    n_c = Bp + Bs
    c_rows = -(-n_c // 16) * 16
    c_all = jnp.pad(jax.nn.silu(jnp.concatenate([c_prompt, c_sample], axis=0)), ((0, c_rows - n_c), (0, 0)))
    w_gate_cols = jnp.pad(w_in[:, :, OFF_NSA_G:OFF_RET], ((0, 0), (0, 0), (0, LANES - 3 * NSA_HEADS)))
    w_tail = w_in[:, :, OFF_RET:]
    w_route = jnp.pad(jnp.concatenate([router_group_w, router_expert_w], axis=2),
                      ((0, 0), (0, 0), (0, LANES - N_GROUPS - N_EXPERTS)))
    half = CMP_STRIDE * HEAD_DIM
    phic = _bf16(jnp.concatenate([cmp_phi[:, :, :half], cmp_phi[:, :, half:]], axis=-1))
    cst = jnp.einsum('lcn,lcne->lce', cmp_pe.reshape(DEPTH, 2, CMP_BLOCK * HEAD_DIM), cmp_phi)[:, :, None, :]
    page_rows = PAGE_SIZE * KV_ROW
    cache_cmp = cache_cmp_kv.reshape(DEPTH, -1, page_rows, HEAD_DIM)
    cache_slc = cache_slc_kv.reshape(DEPTH, -1, page_rows, HEAD_DIM)
    wb = state_win_kv.shape[2]
    state_win = state_win_kv.reshape(DEPTH, Bs, wb * KV_ROW, HEAD_DIM)
    win_out = jnp.zeros(state_win.shape, jnp.float32)
    ret_out = jnp.zeros(state_ret.shape, jnp.float32)
    gn = ret_gn_g.reshape(DEPTH, 1, RET_WIDTH)
    outs = {k: [] for k in ("cmp_p", "slc_p", "win_p", "ret_p", "cmp_s", "slc_s")}
    for l in range(DEPTH):
        mods = (_mm(c_all, w_ada, layer=l, tm=c_rows, tn=1024, name="adaln")[:n_c] + b_ada[l]).reshape(
            n_c, N_MOD, 1, D)
        mod_p = [mods[:Bp, i] for i in range(N_MOD)]
        mod_s = [mods[Bp:, i] for i in range(N_MOD)]
        hb = _norm_mod(xp, norm_mix_g[l], mod_p[1], mod_p[0], tb=1, tt=512)
        qb, kv, kvb, gates, rqkv, rg, zgm = _project(hb, w_in, w_gate_cols, w_tail, layer=l, tm=1024)
        pa, pb = _compress_prompt(kv, phic[l], n=128)
        pb = jnp.concatenate([pb[:, 1:], jnp.zeros_like(pb[:, :1])], axis=1)
        ckv = _bf16(pa + pb + jnp.repeat(cst[l], NSA_KV_HEADS, axis=0))
        o_cmp, selm = _cmp_sel(qb, ckv[:NSA_KV_HEADS], ckv[NSA_KV_HEADS:], gates, tq=256)
        o_slc = _flash(qb, kvb, gates, selm, branch=1, mode="sel", tq=256, tk=512)
        o_win = _flash(qb, kvb, gates, None, branch=2, mode="win", tq=256, tk=256)
        y_ret, s_new = _ret_prompt(rqkv, rg, gn[l], chunk=math.gcd(RET_CHUNK, Tp))
        t = _merge((o_cmp, o_slc, o_win), y_ret, zgm, w_branch_nsa, w_branch_ret, layer=l, tm=1024, tn=512)
        xp = _resid_mm(t, w_out, xp, mod_p[2], layer=l, tb=1, tt=1024, tn=512)
        kv6 = kv.reshape(Bp, Tp, 3, 2, NSA_KV_HEADS, HEAD_DIM)
        outs["cmp_p"].append(kv6[:, :, 0])
        outs["slc_p"].append(kv6[:, :, 1])
        outs["win_p"].append(kv6[:, Tp - win_p:, 2])
        outs["ret_p"].append(s_new[None])
        hb = _norm_mod(xs, norm_mix_g[l], mod_s[1], mod_s[0], tb=32, tt=Ts)
        qb, kv, kvb, gates, rqkv, rg, zgm = _project(hb, w_in, w_gate_cols, w_tail, layer=l, tm=Ns)
        o_nsa, win_out = _nsa_sample(qb.reshape(Bs, Ts, NSA_WIDTH), kv, gates, cache_cmp, cache_slc, state_win,
                                     win_out, page_table, phic[l], cst[l], layer=l)
        y_ret, ret_out = _ret_sample(rqkv.reshape(Bs, Ts, 3 * RET_QK), rg, gn[l], state_ret, ret_out, layer=l)
        zero = jnp.zeros_like(o_nsa)
        t = _merge((o_nsa, zero, zero), y_ret, zgm, w_branch_nsa, w_branch_ret, layer=l, tm=Ns, tn=512)
        xs = _resid_mm(t, w_out, xs, mod_s[2], layer=l, tb=Bs, tt=Ts, tn=512)
        kv6 = kv.reshape(Bs, Ts, 3, 2, NSA_KV_HEADS, HEAD_DIM)
        outs["cmp_s"].append(kv6[:, :, 0])
        outs["slc_s"].append(kv6[:, :, 1])
        hp = _norm_mod(xp, norm_ffn_g[l], mod_p[4], mod_p[3], tb=1, tt=512)
        hs = _norm_mod(xs, norm_ffn_g[l], mod_s[4], mod_s[3], tb=32, tt=Ts)
        y = _hier_moe(jnp.concatenate([hp, hs], axis=0), w_route, router_group_b[l], router_expert_b[l],
                      expert_w_gate, expert_w_up, expert_w_down, layer=l)
        xp = xp + mod_p[5] * y[:Np].reshape(Bp, Tp, D)
        xs = xs + mod_s[5] * y[Np:].reshape(Bs, Ts, D)
    y_prompt = _final_norm(xp.reshape(Np, D), norm_final_g, tm=512).reshape(Bp, Tp, D)
    y_sample = _final_norm(xs.reshape(Ns, D), norm_final_g, tm=512).reshape(Bs, Ts, D)
    return (y_prompt, y_sample, jnp.stack(outs["cmp_p"]), jnp.stack(outs["slc_p"]), jnp.stack(outs["win_p"]),
            jnp.stack(outs["ret_p"]), jnp.stack(outs["cmp_s"]), jnp.stack(outs["slc_s"]),
            win_out.reshape(state_win_kv.shape), ret_out)
```

```python
import functools
import math

import numpy as np
import jax
import jax.numpy as jnp
from jax import lax
from jax.experimental import pallas as pl
from jax.experimental.pallas import tpu as pltpu

D_MODEL = 2048
DEPTH = 2
PAGE_SIZE = 128
NSA_HEADS = 8
NSA_KV_HEADS = 2
NSA_GROUP = NSA_HEADS // NSA_KV_HEADS
HEAD_DIM = 128
NSA_WIDTH = NSA_HEADS * HEAD_DIM
CMP_STRIDE = 16
CMP_BLOCK = 2 * CMP_STRIDE
SEL_BLOCK = 64
SEG_PER_SEL = SEL_BLOCK // CMP_STRIDE
N_SELECT = 16
N_LOCAL_SEL = 2
WINDOW = 512
SCALE = HEAD_DIM ** -0.5
LOG2E = math.log2(math.e)
RET_HEADS = 4
RET_DK = 256
RET_DV = 256
RET_QK = RET_HEADS * RET_DK
RET_WIDTH = RET_HEADS * RET_DV
RET_CHUNK = 128
N_GROUPS = 4
EXPERTS_PER_GROUP = 8
N_EXPERTS = N_GROUPS * EXPERTS_PER_GROUP
TOP_K_IN_GROUP = 2
D_EXPERT = 1024
MOE_ROW_CHUNK = 256
N_MOD = 6
EPS = 1e-6
NEG_INF = -1e30
KV_COLS = 3 * 2 * NSA_KV_HEADS * HEAD_DIM
OFF_NSA_KV = NSA_WIDTH
OFF_NSA_G = OFF_NSA_KV + KV_COLS
OFF_RET = OFF_NSA_G + 3 * NSA_HEADS
OFF_MERGE = OFF_RET + 2 * RET_QK + 2 * RET_WIDTH
IN_COLS = OFF_MERGE + 2 * D_MODEL

LANES = 128
SUBLANES = 8
KV_ROW = 2 * NSA_KV_HEADS
VMEM_LIMIT_BYTES = 56 * 1024 * 1024

ALIBI = tuple(tuple(2.0 ** -(g * NSA_GROUP + r + 1) for r in range(NSA_GROUP)) for g in range(NSA_KV_HEADS))


def _cparams(*sem):
    return pltpu.CompilerParams(dimension_semantics=sem, vmem_limit_bytes=VMEM_LIMIT_BYTES)


def _bf16(x):
    return x.astype(jnp.bfloat16)


def _dot(a, b):
    return jnp.dot(a, b, preferred_element_type=jnp.float32)


def _dot_nt(a, b):
    return lax.dot_general(a, b, (((1,), (1,)), ((), ())), preferred_element_type=jnp.float32)


def _dot_tn(a, b):
    return lax.dot_general(a, b, (((0,), (0,)), ((), ())), preferred_element_type=jnp.float32)


def _norm_mod_kernel(x_ref, g_ref, sc_ref, sh_ref, o_ref):
    x = x_ref[...]
    y = x * lax.rsqrt(jnp.mean(x * x, axis=-1, keepdims=True) + EPS)
    y = y * g_ref[...]
    h = y * (1.0 + sc_ref[...]) + sh_ref[...]
    tb, tt, d = x.shape
    o_ref[...] = h.reshape(tb * tt, d).astype(o_ref.dtype)


def _norm_mod(x, g, scale, shift, *, tb, tt):
    B, T, D = x.shape
    return pl.pallas_call(
        _norm_mod_kernel,
        out_shape=jax.ShapeDtypeStruct((B * T, D), jnp.bfloat16),
        grid=(B // tb, T // tt),
        in_specs=[
            pl.BlockSpec((tb, tt, D), lambda b, t: (b, t, 0)),
            pl.BlockSpec((1, 1, D), lambda b, t: (0, 0, 0)),
            pl.BlockSpec((tb, 1, D), lambda b, t: (b, 0, 0)),
            pl.BlockSpec((tb, 1, D), lambda b, t: (b, 0, 0)),
        ],
        out_specs=pl.BlockSpec((tb * tt, D), lambda b, t: (b * (T // tt) + t, 0)),
        compiler_params=_cparams("parallel", "parallel"),
        name="norm_mod",
    )(x, g.reshape(1, 1, D), scale, shift)


def _final_norm_kernel(x_ref, g_ref, o_ref):
    x = x_ref[...]
    y = x * lax.rsqrt(jnp.mean(x * x, axis=-1, keepdims=True) + EPS)
    o_ref[...] = y * g_ref[...]


def _final_norm(x2, g, *, tm):
    M, D = x2.shape
    return pl.pallas_call(
        _final_norm_kernel,
        out_shape=jax.ShapeDtypeStruct((M, D), jnp.float32),
        grid=(M // tm,),
        in_specs=[pl.BlockSpec((tm, D), lambda i: (i, 0)), pl.BlockSpec((1, D), lambda i: (0, 0))],
        out_specs=pl.BlockSpec((tm, D), lambda i: (i, 0)),
        compiler_params=_cparams("parallel"),
        name="final_norm",
    )(x2, g.reshape(1, D))


def _mm_kernel(x_ref, w_ref, *o_refs, act):
    acc = _dot(_bf16(x_ref[...]), _bf16(w_ref[0]))
    if act == "sigmoid":
        acc = jax.nn.sigmoid(acc)
    elif act == "silu":
        acc = acc * jax.nn.sigmoid(acc)
    for o_ref in o_refs:
        o_ref[...] = acc.astype(o_ref.dtype)


def _mm(x, w3, *, layer=0, col0=0, ncols=None, tm, tn, out_dtypes=(jnp.float32,), act=None, name="mm"):
    M, K = x.shape
    ncols = w3.shape[2] - col0 if ncols is None else ncols
    cb = col0 // tn
    outs = pl.pallas_call(
        functools.partial(_mm_kernel, act=act),
        out_shape=tuple(jax.ShapeDtypeStruct((M, ncols), dt) for dt in out_dtypes),
        grid=(M // tm, ncols // tn),
        in_specs=[pl.BlockSpec((tm, K), lambda i, j: (i, 0)),
                  pl.BlockSpec((1, K, tn), lambda i, j: (layer, 0, cb + j))],
        out_specs=tuple(pl.BlockSpec((tm, tn), lambda i, j: (i, j)) for _ in out_dtypes),
        compiler_params=_cparams("parallel", "parallel"),
        name=name,
    )(x, w3)
    return outs[0] if len(outs) == 1 else outs


def _proj_kv_kernel(x_ref, w_ref, kvb_ref, rows_ref, *, tm):
    acc = _dot(x_ref[...], _bf16(w_ref[0]))
    kvb_ref[...] = _bf16(acc)
    for j in range(KV_ROW):
        rows_ref[0, pl.ds(j, tm, stride=KV_ROW), :] = acc[:, j * HEAD_DIM:(j + 1) * HEAD_DIM]


def _proj_kv(hb, w_in, *, layer, tm):
    M, K = hb.shape
    tn = KV_ROW * HEAD_DIM
    cb = OFF_NSA_KV // tn
    return pl.pallas_call(
        functools.partial(_proj_kv_kernel, tm=tm),
        out_shape=(jax.ShapeDtypeStruct((M, KV_COLS), jnp.bfloat16),
                   jax.ShapeDtypeStruct((3, M * KV_ROW, HEAD_DIM), jnp.float32)),
        grid=(M // tm, 3),
        in_specs=[pl.BlockSpec((tm, K), lambda i, j: (i, 0)),
                  pl.BlockSpec((1, K, tn), lambda i, j: (layer, 0, cb + j))],
        out_specs=(pl.BlockSpec((tm, tn), lambda i, j: (i, j)),
                   pl.BlockSpec((1, tm * KV_ROW, HEAD_DIM), lambda i, j: (j, i, 0))),
        compiler_params=_cparams("parallel", "parallel"),
        name="proj_kv_rows",
    )(hb, w_in)


def _merge_kernel(a0_ref, a1_ref, a2_ref, b_ref, g0_ref, g1_ref, wa_ref, wb_ref, o_ref):
    a = a0_ref[...] + a1_ref[...] + a2_ref[...]
    pa = _dot(_bf16(a), _bf16(wa_ref[0]))
    pb = _dot(_bf16(b_ref[...]), _bf16(wb_ref[0]))
    o_ref[...] = (jax.nn.sigmoid(g0_ref[...]) * pa + jax.nn.sigmoid(g1_ref[...]) * pb).astype(o_ref.dtype)


def _merge(o3, y_ret, zgm, w_a, w_b, *, layer, tm, tn):
    M, K = y_ret.shape
    N = w_a.shape[2]
    nj = N // tn
    row = pl.BlockSpec((tm, K), lambda i, j: (i, 0))
    return pl.pallas_call(
        _merge_kernel,
        out_shape=jax.ShapeDtypeStruct((M, N), jnp.bfloat16),
        grid=(M // tm, nj),
        in_specs=[row, row, row, row,
                  pl.BlockSpec((tm, tn), lambda i, j: (i, j)),
                  pl.BlockSpec((tm, tn), lambda i, j: (i, j + nj)),
                  pl.BlockSpec((1, K, tn), lambda i, j: (layer, 0, j)),
                  pl.BlockSpec((1, K, tn), lambda i, j: (layer, 0, j))],
        out_specs=pl.BlockSpec((tm, tn), lambda i, j: (i, j)),
        compiler_params=_cparams("parallel", "parallel"),
        name="merge_branches",
    )(*o3, y_ret, zgm, zgm, w_a, w_b)


def _resid_mm_kernel(t_ref, w_ref, x_ref, gt_ref, o_ref):
    acc = _dot(t_ref[...], _bf16(w_ref[0]))
    tb, tt, tn = x_ref.shape
    o_ref[...] = x_ref[...] + gt_ref[...] * acc.reshape(tb, tt, tn)


def _resid_mm(t, w3, x, gate, *, layer, tb, tt, tn):
    B, T, N = x.shape
    K = t.shape[1]
    nt = T // tt
    return pl.pallas_call(
        _resid_mm_kernel,
        out_shape=jax.ShapeDtypeStruct((B, T, N), jnp.float32),
        grid=(B // tb, nt, N // tn),
        in_specs=[
            pl.BlockSpec((tb * tt, K), lambda b, t_, j: (b * nt + t_, 0)),
            pl.BlockSpec((1, K, tn), lambda b, t_, j: (layer, 0, j)),
            pl.BlockSpec((tb, tt, tn), lambda b, t_, j: (b, t_, j)),
            pl.BlockSpec((tb, 1, tn), lambda b, t_, j: (b, 0, j)),
        ],
        out_specs=pl.BlockSpec((tb, tt, tn), lambda b, t_, j: (b, t_, j)),
        compiler_params=_cparams("parallel", "parallel", "parallel"),
        name="resid_out_proj",
    )(t, w3, x, gate)


def _sel_matrix_t(nb, ns, nb_pad, ns_pad):
    j = np.arange(ns_pad)[:, None]
    i = np.arange(nb_pad)[None, :]
    lo, hi = SEG_PER_SEL * j, SEG_PER_SEL * j + SEG_PER_SEL - 1
    m = ((i >= lo) & (i <= hi)).astype(np.float32) + ((i + 1 >= lo) & (i + 1 <= hi)).astype(np.float32)
    m = m * ((i < nb) & (j < ns))
    return jnp.asarray(m, jnp.bfloat16)


def _block_importance_t(imp, msel_t):
    hi = _bf16(imp)
    r1 = imp - hi.astype(jnp.float32)
    mid = _bf16(r1)
    lo = _bf16(r1 - mid.astype(jnp.float32))
    return _dot_nt(msel_t, hi) + _dot_nt(msel_t, mid) + _dot_nt(msel_t, lo)


def _pick_blocks_t(sel_t, qpos, ns):
    nsp = sel_t.shape[0]
    j = lax.broadcasted_iota(jnp.int32, (nsp, 1), 0)
    cur = qpos // SEL_BLOCK
    valid = (j <= cur) & (j < ns)
    forced = (j == 0) | (valid & (j > cur - N_LOCAL_SEL))
    score = jnp.where(forced, 1e9, jnp.where(valid, sel_t, -1e9))
    score = jnp.where(j < ns, score, -3e38)
    idx = j.astype(jnp.float32)
    chosen = jnp.zeros(sel_t.shape, jnp.float32)
    for _ in range(min(N_SELECT, ns)):
        m = jnp.max(score, axis=0, keepdims=True)
        first = jnp.min(jnp.where(score == m, idx, float(nsp)), axis=0, keepdims=True)
        hit = idx == first
        chosen = jnp.where(hit, 1.0, chosen)
        score = jnp.where(hit, -3e38, score)
    return jnp.where(valid, chosen, 0.0)


def _softmax_rows(s, mask):
    s = jnp.where(mask, s, NEG_INF)
    e = jnp.exp(s - jnp.max(s, axis=-1, keepdims=True))
    return jnp.where(mask, e * (1.0 / jnp.sum(e, axis=-1, keepdims=True)), 0.0)


def _compress_kernel(rows_ref, phic_ref, a_ref, b_ref, *, n):
    seg = KV_ROW * CMP_STRIDE
    for j in range(KV_ROW):
        x = jnp.concatenate([rows_ref[0, pl.ds(KV_ROW * l + j, n, stride=seg), :] for l in range(CMP_STRIDE)],
                            axis=1)
        y = _dot(_bf16(x), phic_ref[j // NSA_KV_HEADS])
        a_ref[j] = y[:, :HEAD_DIM]
        b_ref[j] = y[:, HEAD_DIM:]


def _compress_prompt(rows, phic, *, n):
    nseg = rows.shape[1] // (KV_ROW * CMP_STRIDE)
    out = jax.ShapeDtypeStruct((KV_ROW, nseg, HEAD_DIM), jnp.float32)
    return pl.pallas_call(
        functools.partial(_compress_kernel, n=n),
        out_shape=(out, out),
        grid=(nseg // n,),
        in_specs=[pl.BlockSpec((1, n * KV_ROW * CMP_STRIDE, HEAD_DIM), lambda i: (0, i, 0)),
                  pl.BlockSpec(phic.shape, lambda i: (0, 0, 0))],
        out_specs=(pl.BlockSpec((KV_ROW, n, HEAD_DIM), lambda i: (0, i, 0)),
                   pl.BlockSpec((KV_ROW, n, HEAD_DIM), lambda i: (0, i, 0))),
        compiler_params=_cparams("parallel"),
        name="nsa_compress",
    )(rows, phic)


def _gate_col(gate_ref, g, branch, r):
    c0 = branch * NSA_HEADS + r
    c1 = c0 + NSA_GROUP
    return jnp.where(g == 0, gate_ref[:, c0:c0 + 1], gate_ref[:, c1:c1 + 1])


def _cmp_sel_kernel(q_ref, kc_ref, vc_ref, msel_ref, gate_ref, o_ref, sel_ref, *, tq, nb, ns):
    g = pl.program_id(0)
    qi = pl.program_id(1)
    qpos = qi * tq + lax.broadcasted_iota(jnp.int32, (tq, 1), 0)
    endpos = lax.broadcasted_iota(jnp.int32, (1, nb), 1) * CMP_STRIDE + (CMP_BLOCK - 1)
    dist = qpos - endpos
    mask = dist >= 0
    distf = dist.astype(jnp.float32)
    kc = kc_ref[0]
    vc = vc_ref[0]
    imp = jnp.zeros((tq, nb), jnp.float32)
    for r in range(NSA_GROUP):
        slope = jnp.where(g == 0, ALIBI[0][r], ALIBI[1][r])
        s = _dot_nt(q_ref[:, r * HEAD_DIM:(r + 1) * HEAD_DIM], kc) * SCALE - slope * distf
        p = _softmax_rows(s, mask)
        o_ref[:, r * HEAD_DIM:(r + 1) * HEAD_DIM] = _gate_col(gate_ref, g, 0, r) * _dot(_bf16(p), vc)
        imp = imp + p
    sel_t = _block_importance_t(imp, msel_ref[...])
    qpos_row = qi * tq + lax.broadcasted_iota(jnp.int32, (1, tq), 1)
    sel_ref[0] = _pick_blocks_t(sel_t, qpos_row, ns).T.astype(sel_ref.dtype)


def _cmp_sel(qb, kc, vc, gates, *, tq):
    T = qb.shape[0]
    nb = kc.shape[1]
    ns = T // SEL_BLOCK
    gw = NSA_GROUP * HEAD_DIM
    return pl.pallas_call(
        functools.partial(_cmp_sel_kernel, tq=tq, nb=nb, ns=ns),
        out_shape=(jax.ShapeDtypeStruct((T, NSA_WIDTH), jnp.float32),
                   jax.ShapeDtypeStruct((NSA_KV_HEADS, T, ns), jnp.bfloat16)),
        grid=(NSA_KV_HEADS, T // tq),
        in_specs=[
            pl.BlockSpec((tq, gw), lambda g, i: (i, g)),
            pl.BlockSpec((1, nb, HEAD_DIM), lambda g, i: (g, 0, 0)),
            pl.BlockSpec((1, nb, HEAD_DIM), lambda g, i: (g, 0, 0)),
            pl.BlockSpec((ns, nb), lambda g, i: (0, 0)),
            pl.BlockSpec((tq, LANES), lambda g, i: (i, 0)),
        ],
        out_specs=(pl.BlockSpec((tq, gw), lambda g, i: (i, g)),
                   pl.BlockSpec((1, tq, ns), lambda g, i: (g, i, 0))),
        compiler_params=_cparams("parallel", "parallel"),
        name="nsa_cmp_select",
    )(qb, kc, vc, _sel_matrix_t(nb, ns, nb, ns), gates)


def _scores_log2(q, k, msk, slope, krel):
    z = jnp.where(msk, _dot_nt(q, k) * (SCALE * LOG2E) + (slope * LOG2E) * krel, NEG_INF)
    return [z[:, c * LANES:(c + 1) * LANES] for c in range(z.shape[1] // LANES)]


def _lane_fold(chunks, op):
    acc = chunks[0]
    for c in chunks[1:]:
        acc = op(acc, c)
    return acc


def _sel_attn_kernel(hit_ref, q_ref, k_ref, v_ref, gate_ref, sel_ref, o_ref, m_scr, l_scr, acc_scr,
                     *, tq, tk, nk, branch):
    g = pl.program_id(0)
    qi = pl.program_id(1)
    kj = pl.program_id(2)

    @pl.when(kj == 0)
    def _():
        m_scr[...] = jnp.full(m_scr.shape, NEG_INF, jnp.float32)
        l_scr[...] = jnp.zeros(l_scr.shape, jnp.float32)
        acc_scr[...] = jnp.zeros(acc_scr.shape, jnp.float32)

    touched = hit_ref[(g * pl.num_programs(1) + qi) * nk + kj] > 0

    @pl.when((kj * tk <= qi * tq + (tq - 1)) & touched)
    def _():
        k = k_ref[...]
        v = v_ref[...]
        kpos = kj * tk + lax.broadcasted_iota(jnp.int32, (1, tk), 1)
        causal = (qi * tq + lax.broadcasted_iota(jnp.int32, (tq, 1), 0)) >= kpos
        blk = lax.broadcasted_iota(jnp.int32, (sel_ref.shape[2], 1), 0)
        expand = _bf16(kpos // SEL_BLOCK == blk)
        msk = (_dot(sel_ref[0], expand) > 0.5) & causal
        krel = (kpos - qi * tq).astype(jnp.float32)
        for r in range(NSA_GROUP):
            slope = jnp.where(g == 0, ALIBI[0][r], ALIBI[1][r])
            zc = _scores_log2(q_ref[:, r * HEAD_DIM:(r + 1) * HEAD_DIM], k, msk, slope, krel)
            m_prev = m_scr[r]
            m_new = jnp.maximum(m_prev, jnp.max(_lane_fold(zc, jnp.maximum), axis=-1, keepdims=True))
            alpha = jnp.exp2(m_prev - m_new)
            pc = [jnp.exp2(z - m_new) for z in zc]
            l_scr[r] = alpha * l_scr[r] + jnp.sum(_lane_fold(pc, jnp.add), axis=-1, keepdims=True)
            acc_scr[r] = alpha * acc_scr[r] + _dot(_bf16(jnp.concatenate(pc, axis=1)), v)
            m_scr[r] = m_new

    @pl.when(kj == nk - 1)
    def _():
        for r in range(NSA_GROUP):
            o_ref[:, r * HEAD_DIM:(r + 1) * HEAD_DIM] = (
                acc_scr[r] * (_gate_col(gate_ref, g, branch, r) / l_scr[r]))


def _sel_attn(qb, kvb, gates, selm, *, branch, tq, tk):
    T = qb.shape[0]
    gw = NSA_GROUP * HEAD_DIM
    kcol = (branch * 2 + 0) * NSA_KV_HEADS
    vcol = (branch * 2 + 1) * NSA_KV_HEADS
    nk = T // tk
    ns = selm.shape[2]

    def ktile(i, j):
        return jnp.minimum(j, (i * tq + tq - 1) // tk)

    hit = jnp.max(selm.reshape(NSA_KV_HEADS, T // tq, tq, nk, tk // SEL_BLOCK).astype(jnp.float32), axis=(2, 4))
    hit = (hit > 0).astype(jnp.int32).reshape(-1)
    grid_spec = pltpu.PrefetchScalarGridSpec(
        num_scalar_prefetch=1,
        grid=(NSA_KV_HEADS, T // tq, nk),
        in_specs=[
            pl.BlockSpec((tq, gw), lambda g, i, j, h: (i, g)),
            pl.BlockSpec((tk, HEAD_DIM), lambda g, i, j, h: (ktile(i, j), kcol + g)),
            pl.BlockSpec((tk, HEAD_DIM), lambda g, i, j, h: (ktile(i, j), vcol + g)),
            pl.BlockSpec((tq, LANES), lambda g, i, j, h: (i, 0)),
            pl.BlockSpec((1, tq, ns), lambda g, i, j, h: (g, i, 0)),
        ],
        out_specs=pl.BlockSpec((tq, gw), lambda g, i, j, h: (i, g)),
        scratch_shapes=[pltpu.VMEM((NSA_GROUP, tq, LANES), jnp.float32),
                        pltpu.VMEM((NSA_GROUP, tq, LANES), jnp.float32),
                        pltpu.VMEM((NSA_GROUP, tq, HEAD_DIM), jnp.float32)],
    )
    return pl.pallas_call(
        functools.partial(_sel_attn_kernel, tq=tq, tk=tk, nk=nk, branch=branch),
        out_shape=jax.ShapeDtypeStruct((T, NSA_WIDTH), jnp.float32),
        grid_spec=grid_spec,
        compiler_params=_cparams("parallel", "parallel", "arbitrary"),
        name="nsa_sel_attn",
    )(hit, qb, kvb, kvb, gates, selm)


def _win_attn_kernel(q_ref, *refs, tq, n_prev, branch):
    k_refs = refs[:n_prev + 1]
    v_refs = refs[n_prev + 1:2 * (n_prev + 1)]
    gate_ref, o_ref = refs[2 * (n_prev + 1):]
    g = pl.program_id(0)
    qi = pl.program_id(1)
    k = jnp.concatenate([r[...] for r in k_refs], axis=0)
    v = jnp.concatenate([r[...] for r in v_refs], axis=0)
    kpos = (qi - n_prev) * tq + lax.broadcasted_iota(jnp.int32, (1, k.shape[0]), 1)
    dist = (qi * tq + lax.broadcasted_iota(jnp.int32, (tq, 1), 0)) - kpos
    msk = (dist >= 0) & (dist < WINDOW) & (kpos >= 0)
    krel = (kpos - qi * tq).astype(jnp.float32)
    for r in range(NSA_GROUP):
        slope = jnp.where(g == 0, ALIBI[0][r], ALIBI[1][r])
        zc = _scores_log2(q_ref[:, r * HEAD_DIM:(r + 1) * HEAD_DIM], k, msk, slope, krel)
        m = jnp.max(_lane_fold(zc, jnp.maximum), axis=-1, keepdims=True)
        pc = [jnp.exp2(z - m) for z in zc]
        l = jnp.sum(_lane_fold(pc, jnp.add), axis=-1, keepdims=True)
        o = _dot(_bf16(jnp.concatenate(pc, axis=1)), v)
        o_ref[:, r * HEAD_DIM:(r + 1) * HEAD_DIM] = o * (_gate_col(gate_ref, g, branch, r) / l)


def _win_attn(qb, kvb, gates, *, branch, tq):
    T = qb.shape[0]
    gw = NSA_GROUP * HEAD_DIM
    n_prev = WINDOW // tq
    kcol = (branch * 2 + 0) * NSA_KV_HEADS
    vcol = (branch * 2 + 1) * NSA_KV_HEADS

    def tile_spec(col, t):
        return pl.BlockSpec((tq, HEAD_DIM), lambda g, i: (jnp.maximum(i - n_prev + t, 0), col + g))

    return pl.pallas_call(
        functools.partial(_win_attn_kernel, tq=tq, n_prev=n_prev, branch=branch),
        out_shape=jax.ShapeDtypeStruct((T, NSA_WIDTH), jnp.float32),
        grid=(NSA_KV_HEADS, T // tq),
        in_specs=[pl.BlockSpec((tq, gw), lambda g, i: (i, g))]
        + [tile_spec(kcol, t) for t in range(n_prev + 1)] + [tile_spec(vcol, t) for t in range(n_prev + 1)]
        + [pl.BlockSpec((tq, LANES), lambda g, i: (i, 0))],
        out_specs=pl.BlockSpec((tq, gw), lambda g, i: (i, g)),
        compiler_params=_cparams("parallel", "parallel"),
        name="nsa_win_attn",
    )(qb, *([kvb] * (2 * (n_prev + 1))), gates)


def _nsa_sample_kernel(pt_ref, q_ref, kvn_ref, gate_ref, *rest, n_pages, tq, nb, ns, lk, lw):
    cmp_refs = rest[:n_pages]
    slc_refs = rest[n_pages:2 * n_pages]
    win_ref, phic_ref, cst_ref, msel_ref, perm_ref, o_ref, wout_ref, x_scr, k_scr, v_scr = rest[2 * n_pages:]
    del pt_ref
    past = n_pages * PAGE_SIZE
    wb = win_ref.shape[2] // KV_ROW
    rows = NSA_GROUP * tq
    row = lax.broadcasted_iota(jnp.int32, (rows, 1), 0)
    qpos = past + (row % tq)
    rsel = row // tq
    qf = q_ref[0].astype(jnp.float32)
    nbp = x_scr.shape[1]
    zero_tail = jnp.zeros((LANES - tq, HEAD_DIM), jnp.float32)

    def kv_new(branch, c, g):
        col = ((branch * 2 + c) * NSA_KV_HEADS + g) * HEAD_DIM
        return kvn_ref[:, col:col + HEAD_DIM]

    def gated(o, g, branch):
        for r in range(NSA_GROUP):
            c = branch * NSA_HEADS + g * NSA_GROUP + r
            col = (g * NSA_GROUP + r) * HEAD_DIM
            val = gate_ref[:, c:c + 1] * o[r * tq:(r + 1) * tq]
            if branch == 0:
                o_ref[:, col:col + HEAD_DIM] = val
            else:
                o_ref[:, col:col + HEAD_DIM] += val

    for p in range(n_pages):
        y = _dot(perm_ref[...], _bf16(cmp_refs[p][0, 0]))
        for j in range(KV_ROW):
            for l in range(CMP_STRIDE):
                r0 = (j * CMP_STRIDE + l) * SUBLANES
                x_scr[j, p * SUBLANES:(p + 1) * SUBLANES, l * HEAD_DIM:(l + 1) * HEAD_DIM] = y[r0:r0 + SUBLANES]
    base = n_pages * SUBLANES
    ckv = [[None, None], [None, None]]
    for c in range(2):
        for g in range(NSA_KV_HEADS):
            j = c * NSA_KV_HEADS + g
            x_scr[j, base:nbp, :] = jnp.zeros((nbp - base, x_scr.shape[2]), jnp.float32)
            new = kv_new(0, c, g)
            for l in range(tq):
                x_scr[j, base:base + 1, l * HEAD_DIM:(l + 1) * HEAD_DIM] = new[l:l + 1]
            y = _dot(_bf16(x_scr[j]), phic_ref[c])
            second = jnp.concatenate([y[1:, HEAD_DIM:], jnp.zeros((1, HEAD_DIM), jnp.float32)], axis=0)
            ckv[c][g] = _bf16(y[:, :HEAD_DIM] + second + cst_ref[c])

    endpos = lax.broadcasted_iota(jnp.int32, (1, nbp), 1) * CMP_STRIDE + (CMP_BLOCK - 1)
    dist_c = qpos - endpos
    mask_c = (dist_c >= 0) & (lax.broadcasted_iota(jnp.int32, (1, nbp), 1) < nb)
    kpos_s = lax.broadcasted_iota(jnp.int32, (1, lk), 1)
    dist_s = qpos - kpos_s
    blk = lax.broadcasted_iota(jnp.int32, (LANES, 1), 0)
    expand = _bf16(kpos_s // SEL_BLOCK == blk)
    kpos_w = past - wb + lax.broadcasted_iota(jnp.int32, (1, lw), 1)
    dist_w = qpos - kpos_w
    mask_w = (dist_w >= 0) & (dist_w < WINDOW)

    slopes, qgs, imps = [], [], []
    for g in range(NSA_KV_HEADS):
        slope = jnp.zeros((rows, 1), jnp.float32)
        for r in range(NSA_GROUP):
            slope = jnp.where(rsel == r, ALIBI[g][r], slope)
        qg = _bf16(jnp.concatenate(
            [qf[:, (g * NSA_GROUP + r) * HEAD_DIM:(g * NSA_GROUP + r + 1) * HEAD_DIM] for r in range(NSA_GROUP)],
            axis=0))
        slopes.append(slope)
        qgs.append(qg)
        s = _dot_nt(qg, ckv[0][g]) * SCALE - slope * dist_c.astype(jnp.float32)
        p = _softmax_rows(s, mask_c)
        gated(_dot(_bf16(p), ckv[1][g]), g, 0)
        imp = p[0:tq]
        for r in range(1, NSA_GROUP):
            imp = imp + p[r * tq:(r + 1) * tq]
        imps.append(imp)

    imp_all = jnp.concatenate(imps + [jnp.zeros((LANES - NSA_KV_HEADS * tq, nbp), jnp.float32)], axis=0)
    sel_t = _block_importance_t(imp_all, msel_ref[...])
    qpos_row = past + lax.broadcasted_iota(jnp.int32, (1, LANES), 1) % tq
    chosen = _pick_blocks_t(sel_t, qpos_row, ns).T
    keep_all = _dot(_bf16(chosen[0:2 * SUBLANES]), expand)

    for g in range(NSA_KV_HEADS):
        slope, qg = slopes[g], qgs[g]
        for p_i in range(n_pages):
            k_scr[g, p_i * PAGE_SIZE:(p_i + 1) * PAGE_SIZE, :] = _bf16(
                slc_refs[p_i][0, 0, pl.ds(g, PAGE_SIZE, stride=KV_ROW), :])
            v_scr[g, p_i * PAGE_SIZE:(p_i + 1) * PAGE_SIZE, :] = _bf16(
                slc_refs[p_i][0, 0, pl.ds(NSA_KV_HEADS + g, PAGE_SIZE, stride=KV_ROW), :])
        k_scr[g, past:lk, :] = _bf16(jnp.concatenate([kv_new(1, 0, g), zero_tail], axis=0))
        v_scr[g, past:lk, :] = _bf16(jnp.concatenate([kv_new(1, 1, g), zero_tail], axis=0))
        keep = jnp.concatenate([keep_all[g * tq:(g + 1) * tq]] * NSA_GROUP, axis=0)
        s = _dot_nt(qg, k_scr[g]) * SCALE - slope * dist_s.astype(jnp.float32)
        p = _softmax_rows(s, (keep > 0.5) & (dist_s >= 0))
        gated(_dot(_bf16(p), v_scr[g]), g, 1)
        kw = jnp.concatenate([win_ref[0, 0, pl.ds(g, wb, stride=KV_ROW), :], kv_new(2, 0, g), zero_tail], axis=0)
        vw = jnp.concatenate([win_ref[0, 0, pl.ds(NSA_KV_HEADS + g, wb, stride=KV_ROW), :], kv_new(2, 1, g),
                              zero_tail], axis=0)
        s = _dot_nt(qg, _bf16(kw)) * SCALE - slope * dist_w.astype(jnp.float32)
        p = _softmax_rows(s, mask_w)
        gated(_dot(_bf16(p), _bf16(vw)), g, 2)

    keep_rows = (wb - tq) * KV_ROW
    wout_ref[0, 0, 0:keep_rows, :] = win_ref[0, 0, tq * KV_ROW:wb * KV_ROW, :]
    for t in range(tq):
        for j in range(KV_ROW):
            col = (2 * KV_ROW + j) * HEAD_DIM
            wout_ref[0, 0, keep_rows + t * KV_ROW + j:keep_rows + t * KV_ROW + j + 1, :] = (
                kvn_ref[t:t + 1, col:col + HEAD_DIM])


def _nsa_sample(q3, kvn, gates, cache_cmp, cache_slc, state_win, win_out, page_table, phic, cst, *, layer):
    Bd, Tq, _ = q3.shape
    n_pages = page_table.shape[1]
    past = n_pages * PAGE_SIZE
    total = past + Tq
    l_pad = -(-total // SEL_BLOCK) * SEL_BLOCK
    nb = l_pad // CMP_STRIDE
    ns = l_pad // SEL_BLOCK
    nbp = -(-nb // SUBLANES) * SUBLANES
    lk = past + LANES
    wrows = state_win.shape[2]
    lw = wrows // KV_ROW + LANES
    page_rows = PAGE_SIZE * KV_ROW

    def page_spec(p):
        return pl.BlockSpec((1, 1, page_rows, HEAD_DIM), lambda b, pt: (layer, pt[b * n_pages + p], 0, 0))

    grid_spec = pltpu.PrefetchScalarGridSpec(
        num_scalar_prefetch=1,
        grid=(Bd,),
        in_specs=[pl.BlockSpec((1, Tq, NSA_WIDTH), lambda b, pt: (b, 0, 0)),
                  pl.BlockSpec((Tq, KV_COLS), lambda b, pt: (b, 0)),
                  pl.BlockSpec((Tq, LANES), lambda b, pt: (b, 0))]
        + [page_spec(p) for p in range(n_pages)] + [page_spec(p) for p in range(n_pages)]
        + [pl.BlockSpec((1, 1, wrows, HEAD_DIM), lambda b, pt: (layer, b, 0, 0)),
           pl.BlockSpec(phic.shape, lambda b, pt: (0, 0, 0)),
           pl.BlockSpec(cst.shape, lambda b, pt: (0, 0, 0)),
           pl.BlockSpec((LANES, nbp), lambda b, pt: (0, 0)),
           pl.BlockSpec((page_rows, page_rows), lambda b, pt: (0, 0)),
           pl.BlockSpec(memory_space=pl.ANY)],
        out_specs=(pl.BlockSpec((Tq, NSA_WIDTH), lambda b, pt: (b, 0)),
                   pl.BlockSpec((1, 1, wrows, HEAD_DIM), lambda b, pt: (layer, b, 0, 0))),
        scratch_shapes=[pltpu.VMEM((KV_ROW, nbp, CMP_STRIDE * HEAD_DIM), jnp.float32),
                        pltpu.VMEM((NSA_KV_HEADS, lk, HEAD_DIM), jnp.bfloat16),
                        pltpu.VMEM((NSA_KV_HEADS, lk, HEAD_DIM), jnp.bfloat16)],
    )
    n_in = 3 + 2 * n_pages + 5
    src = np.arange(page_rows)
    i_, l_, j_ = src // (KV_ROW * CMP_STRIDE), (src // KV_ROW) % CMP_STRIDE, src % KV_ROW
    perm = np.zeros((page_rows, page_rows), np.float32)
    perm[(j_ * CMP_STRIDE + l_) * (PAGE_SIZE // CMP_STRIDE) + i_, src] = 1.0

    def body(pt_ref, *refs):
        ins = refs[:n_in]
        outs_scr = refs[n_in + 1:]
        _nsa_sample_kernel(pt_ref, *ins, *outs_scr, n_pages=n_pages, tq=Tq, nb=nb, ns=ns, lk=lk, lw=lw)

    return pl.pallas_call(
        body,
        out_shape=(jax.ShapeDtypeStruct((Bd * Tq, NSA_WIDTH), jnp.float32),
                   jax.ShapeDtypeStruct(win_out.shape, win_out.dtype)),
        grid_spec=grid_spec,
        input_output_aliases={n_in + 1: 1},
        compiler_params=_cparams("arbitrary"),
        name="nsa_sample",
    )(page_table.reshape(-1), q3, kvn, gates, *([cache_cmp] * n_pages), *([cache_slc] * n_pages),
      state_win, phic, cst, _sel_matrix_t(nb, ns, nbp, LANES), jnp.asarray(perm, jnp.bfloat16), win_out)


def _decay_tables(chunk):
    log_gamma = jnp.log1p(-jnp.exp2(-5.0 - jnp.arange(RET_HEADS, dtype=jnp.float32)))
    i = jnp.arange(chunk, dtype=jnp.float32)
    diff = i[:, None] - i[None, :]
    d_in = jnp.where(diff >= 0, jnp.exp(jnp.maximum(diff, 0.0)[None] * log_gamma[:, None, None]), 0.0)
    d_q = jnp.exp((i + 1.0)[None, :] * log_gamma[:, None])[:, :, None]
    d_k = jnp.exp((chunk - 1.0 - i)[None, :] * log_gamma[:, None])[:, :, None]
    d_c = jnp.exp(chunk * log_gamma)[:, None, None]
    return d_in, d_q, d_k, d_c


def _ret_head(q, k, v, s_prev, d_in, d_q, d_k, d_c):
    kf = k * RET_DK ** -0.5
    att = _dot_nt(_bf16(q), _bf16(kf)) * d_in
    o = _dot(_bf16(att), _bf16(v)) + _dot(_bf16(q), _bf16(s_prev)) * d_q
    s_new = s_prev * d_c + _dot_tn(_bf16(kf * d_k), _bf16(v))
    return o, s_new


def _group_norm_gate(o, gn, rg):
    mu = jnp.mean(o, axis=-1, keepdims=True)
    var = jnp.mean(jnp.square(o - mu), axis=-1, keepdims=True)
    return rg * (((o - mu) * lax.rsqrt(var + EPS)) * gn)


def _ret_prompt_kernel(q_ref, k_ref, v_ref, rg_ref, gn_ref, din_ref, dq_ref, dk_ref, dc_ref,
                       y_ref, s_ref, st_scr, *, nc):
    c = pl.program_id(0)

    @pl.when(c == 0)
    def _():
        st_scr[...] = jnp.zeros(st_scr.shape, jnp.float32)

    for h in range(RET_HEADS):
        sl = slice(h * RET_DK, (h + 1) * RET_DK)
        o, s_new = _ret_head(q_ref[:, sl].astype(jnp.float32), k_ref[:, sl].astype(jnp.float32),
                             v_ref[:, sl].astype(jnp.float32), st_scr[h],
                             din_ref[h], dq_ref[h], dk_ref[h], dc_ref[h])
        st_scr[h] = s_new
        y_ref[:, sl] = _group_norm_gate(o, gn_ref[:, sl], rg_ref[:, sl]).astype(y_ref.dtype)

    @pl.when(c == nc - 1)
    def _():
        s_ref[...] = st_scr[...]


def _ret_prompt(rqkv, rg, gn, *, chunk):
    T = rqkv.shape[0]
    nc = T // chunk
    tabs = _decay_tables(chunk)
    full = lambda a: pl.BlockSpec(a.shape, lambda c: (0,) * a.ndim)
    return pl.pallas_call(
        functools.partial(_ret_prompt_kernel, nc=nc),
        out_shape=(jax.ShapeDtypeStruct((T, RET_WIDTH), jnp.bfloat16),
                   jax.ShapeDtypeStruct((RET_HEADS, RET_DK, RET_DV), jnp.float32)),
        grid=(nc,),
        in_specs=[pl.BlockSpec((chunk, RET_QK), lambda c: (c, 0)),
                  pl.BlockSpec((chunk, RET_QK), lambda c: (c, 1)),
                  pl.BlockSpec((chunk, RET_WIDTH), lambda c: (c, 2)),
                  pl.BlockSpec((chunk, RET_WIDTH), lambda c: (c, 0)),
                  full(gn)] + [full(t) for t in tabs],
        out_specs=(pl.BlockSpec((chunk, RET_WIDTH), lambda c: (c, 0)),
                   pl.BlockSpec((RET_HEADS, RET_DK, RET_DV), lambda c: (0, 0, 0))),
        scratch_shapes=[pltpu.VMEM((RET_HEADS, RET_DK, RET_DV), jnp.float32)],
        compiler_params=_cparams("arbitrary"),
        name="retention_prompt",
    )(rqkv, rqkv, rqkv, rg, gn, *tabs)


def _ret_sample_kernel(qkv_ref, rg_ref, gn_ref, din_ref, dq_ref, dk_ref, dc_ref, s0_ref, _, y_ref, s_ref, *, tq):
    x = qkv_ref[0].astype(jnp.float32)
    x = jnp.concatenate([x, jnp.zeros((2 * SUBLANES - tq, x.shape[1]), jnp.float32)], axis=0)
    for h in range(RET_HEADS):
        sl = slice(h * RET_DK, (h + 1) * RET_DK)
        o, s_new = _ret_head(x[:, h * RET_DK:(h + 1) * RET_DK],
                             x[:, RET_QK + h * RET_DK:RET_QK + (h + 1) * RET_DK],
                             x[:, 2 * RET_QK + h * RET_DV:2 * RET_QK + (h + 1) * RET_DV],
                             s0_ref[0, 0, h], din_ref[h], dq_ref[h], dk_ref[h], dc_ref[h])
        s_ref[0, 0, h] = s_new
        y_ref[:, sl] = _group_norm_gate(o[0:tq], gn_ref[:, sl], rg_ref[:, sl])


def _ret_sample(rqkv3, rg, gn, state, state_out, *, layer):
    Bd, Tq, _ = rqkv3.shape
    pad = 2 * SUBLANES
    d_in, d_q, d_k, d_c = _decay_tables(Tq)
    d_in = jnp.pad(d_in, ((0, 0), (0, pad - Tq), (0, pad - Tq)))
    d_q = jnp.pad(d_q, ((0, 0), (0, pad - Tq), (0, 0)))
    d_k = jnp.pad(d_k, ((0, 0), (0, pad - Tq), (0, 0)))
    tabs = (d_in, d_q, d_k, d_c)
    full = lambda a: pl.BlockSpec(a.shape, lambda b: (0,) * a.ndim)
    st_spec = pl.BlockSpec((1, 1, RET_HEADS, RET_DK, RET_DV), lambda b: (layer, b, 0, 0, 0))
    return pl.pallas_call(
        functools.partial(_ret_sample_kernel, tq=Tq),
        out_shape=(jax.ShapeDtypeStruct((Bd * Tq, RET_WIDTH), jnp.float32),
                   jax.ShapeDtypeStruct(state_out.shape, state_out.dtype)),
        grid=(Bd,),
        in_specs=[pl.BlockSpec((1, Tq, 3 * RET_QK), lambda b: (b, 0, 0)),
                  pl.BlockSpec((Tq, RET_WIDTH), lambda b: (b, 0)),
                  full(gn)] + [full(t) for t in tabs] + [st_spec, pl.BlockSpec(memory_space=pl.ANY)],
        out_specs=(pl.BlockSpec((Tq, RET_WIDTH), lambda b: (b, 0)), st_spec),
        input_output_aliases={8: 1},
        compiler_params=_cparams("arbitrary"),
        name="retention_sample",
    )(rqkv3, rg, gn, *tabs, state, state_out)


def _moe_kernel(tile_ref, exp_ref, lo_ref, hi_ref, first_ref, valid_ref,
                xs_ref, ws_ref, wg_ref, wu_ref, wd_ref, o_ref, wg_scr, wu_scr, wd_scr, *, tm, ch):
    k = pl.program_id(0)
    f = pl.program_id(1)
    del exp_ref

    @pl.when((first_ref[k] == 1) & (f == 0))
    def _():
        o_ref[...] = jnp.zeros(o_ref.shape, jnp.float32)

    @pl.when(valid_ref[k] == 1)
    def _():
        wg_scr[...] = _bf16(wg_ref[0])
        wu_scr[...] = _bf16(wu_ref[0])
        wd_scr[...] = _bf16(wd_ref[0])
        lo = lo_ref[k]
        hi = hi_ref[k]
        for c in range(tm // ch):
            r0 = tile_ref[k] * tm + c * ch

            @pl.when((r0 < hi) & (r0 + ch > lo))
            def _():
                x = xs_ref[c * ch:(c + 1) * ch, :]
                a = _dot(x, wg_scr[...])
                u = _dot(x, wu_scr[...])
                rows = r0 + lax.broadcasted_iota(jnp.int32, (ch, 1), 0)
                mine = (rows >= lo) & (rows < hi)
                h = jnp.where(mine, a * jax.nn.sigmoid(a) * u * ws_ref[c * ch:(c + 1) * ch, :], 0.0)
                o_ref[c * ch:(c + 1) * ch, :] += _dot(_bf16(h), wd_scr[...])


def _moe_experts(xs, ws, meta, w_gate, w_up, w_down, *, layer, tm, tf):
    R, D = xs.shape
    F = w_gate.shape[3]
    nf = F // tf
    n_items = meta[0].shape[0]

    def fidx(k, f, valid):
        return jnp.where(valid[k] == 1, f, nf - 1)

    grid_spec = pltpu.PrefetchScalarGridSpec(
        num_scalar_prefetch=6,
        grid=(n_items, nf),
        in_specs=[
            pl.BlockSpec((tm, D), lambda k, f, t, e, lo, hi, fi, va: (t[k], 0)),
            pl.BlockSpec((tm, 1), lambda k, f, t, e, lo, hi, fi, va: (t[k], 0)),
            pl.BlockSpec((None, 1, D, tf), lambda k, f, t, e, lo, hi, fi, va: (layer, e[k], 0, fidx(k, f, va))),
            pl.BlockSpec((None, 1, D, tf), lambda k, f, t, e, lo, hi, fi, va: (layer, e[k], 0, fidx(k, f, va))),
            pl.BlockSpec((None, 1, tf, D), lambda k, f, t, e, lo, hi, fi, va: (layer, e[k], fidx(k, f, va), 0)),
        ],
        out_specs=pl.BlockSpec((tm, D), lambda k, f, t, e, lo, hi, fi, va: (t[k], 0)),
        scratch_shapes=[pltpu.VMEM((D, tf), jnp.bfloat16), pltpu.VMEM((D, tf), jnp.bfloat16),
                        pltpu.VMEM((tf, D), jnp.bfloat16)],
    )
    return pl.pallas_call(
        functools.partial(_moe_kernel, tm=tm, ch=MOE_ROW_CHUNK),
        out_shape=jax.ShapeDtypeStruct((R, D), jnp.float32),
        grid_spec=grid_spec,
        compiler_params=_cparams("arbitrary", "arbitrary"),
        name="moe_experts",
    )(*meta, xs, ws, w_gate, w_up, w_down)


def _moe_meta(sizes, *, tm, n_tiles):
    n_items = n_tiles + N_EXPERTS - 1
    ends = jnp.cumsum(sizes)
    starts = ends - sizes
    first_tile = starts // tm
    last_tile = jnp.maximum(ends - 1, 0) // tm
    cnt = jnp.where(sizes > 0, last_tile - first_tile + 1, 0)
    cum = jnp.cumsum(cnt)
    total = cum[-1]
    k = jnp.arange(n_items, dtype=jnp.int32)
    kk = jnp.minimum(k, total - 1)
    e = jnp.sum((cum[None, :] <= kk[:, None]).astype(jnp.int32), axis=1)
    tile = first_tile[e] + (kk - (cum[e] - cnt[e]))
    valid = (k < total).astype(jnp.int32)
    first = jnp.concatenate([jnp.ones((1,), jnp.int32), (tile[1:] != tile[:-1]).astype(jnp.int32)]) * valid
    return (tile.astype(jnp.int32), e.astype(jnp.int32), starts[e].astype(jnp.int32), ends[e].astype(jnp.int32),
            first, valid)


def _hier_moe(hb, w_route, rg_b, re_b, w_gate, w_up, w_down, *, layer, tm=1024, tf=256):
    N, D = hb.shape
    n_route = N_GROUPS + N_EXPERTS
    logits = _mm(hb, w_route, layer=layer, tm=1024, tn=LANES, name="moe_router")
    g_logits = logits[:, :N_GROUPS] + rg_b
    e_logits = (logits[:, N_GROUPS:n_route] + re_b).reshape(N, N_GROUPS, EXPERTS_PER_GROUP)
    g_sel = jnp.argmax(g_logits, axis=-1)
    g_w = jnp.take_along_axis(jax.nn.softmax(g_logits, axis=-1), g_sel[:, None], axis=-1)
    e_in = jnp.take_along_axis(e_logits, g_sel[:, None, None], axis=1)[:, 0]
    top_v, top_i = lax.top_k(e_in, TOP_K_IN_GROUP)
    w = jax.nn.softmax(top_v, axis=-1) * g_w
    flat_e = (g_sel[:, None] * EXPERTS_PER_GROUP + top_i).reshape(-1).astype(jnp.int32)
    R = N * TOP_K_IN_GROUP
    onehot = (flat_e[:, None] == jnp.arange(N_EXPERTS, dtype=jnp.int32)[None, :]).astype(jnp.int32)
    csum = jnp.cumsum(onehot, axis=0)
    sizes = csum[-1]
    rank = jnp.sum(csum * onehot, axis=1) - 1
    starts = jnp.cumsum(sizes) - sizes
    dest = starts[flat_e] + rank
    tok = jnp.zeros((R,), jnp.int32).at[dest].set(jnp.arange(R, dtype=jnp.int32) // TOP_K_IN_GROUP)
    ws = jnp.zeros((R,), jnp.float32).at[dest].set(w.reshape(-1))[:, None]
    xs = hb[tok]
    meta = _moe_meta(sizes, tm=tm, n_tiles=R // tm)
    ys = _moe_experts(xs, ws, meta, w_gate, w_up, w_down, layer=layer, tm=tm, tf=tf)
    return ys[dest].reshape(N, TOP_K_IN_GROUP, D).sum(axis=1)


def _project(hb, w_in, w_gate_cols, w_tail, *, layer, tm, kv_rows):
    qb = _mm(hb, w_in, layer=layer, col0=0, ncols=OFF_NSA_KV, tm=tm, tn=512, out_dtypes=(jnp.bfloat16,),
             name="proj_q")
    if kv_rows:
        kvb, kv = _proj_kv(hb, w_in, layer=layer, tm=tm)
    else:
        kv = _mm(hb, w_in, layer=layer, col0=OFF_NSA_KV, ncols=KV_COLS, tm=tm, tn=512, name="proj_kv")
        kvb = None
    gates = _mm(hb, w_gate_cols, layer=layer, tm=tm, tn=LANES, act="sigmoid", name="proj_gate")
    rqkv = _mm(hb, w_tail, layer=layer, col0=0, ncols=3 * RET_QK, tm=tm, tn=512, out_dtypes=(jnp.bfloat16,),
               name="proj_ret")
    rg = _mm(hb, w_tail, layer=layer, col0=3 * RET_QK, ncols=RET_WIDTH, tm=tm, tn=512, act="silu",
             name="proj_ret_gate")
    zgm = _mm(hb, w_tail, layer=layer, col0=3 * RET_QK + RET_WIDTH, ncols=2 * D_MODEL, tm=tm, tn=512,
              name="proj_merge")
    return qb, kv, kvb, gates, rqkv, rg, zgm


def kernel(x_prompt, x_sample, c_prompt, c_sample, cache_cmp_kv, cache_slc_kv, state_win_kv, state_ret, page_table, norm_mix_g, norm_ffn_g, norm_final_g, w_ada, b_ada, w_in, cmp_pe, cmp_phi, ret_gn_g, w_branch_nsa, w_branch_ret, w_out, router_group_w, router_group_b, router_expert_w, router_expert_b, expert_w_gate, expert_w_up, expert_w_down):
    xp, xs = x_prompt, x_sample
    Bp, Tp, D = xp.shape
    Bs, Ts, _ = xs.shape
    Np, Ns = Bp * Tp, Bs * Ts
    win_p = min(WINDOW, Tp)
    n_c = Bp + Bs
    c_rows = -(-n_c // 16) * 16
    c_all = jnp.pad(jax.nn.silu(jnp.concatenate([c_prompt, c_sample], axis=0)), ((0, c_rows - n_c), (0, 0)))
    w_gate_cols = jnp.pad(w_in[:, :, OFF_NSA_G:OFF_RET], ((0, 0), (0, 0), (0, LANES - 3 * NSA_HEADS)))
    w_tail = w_in[:, :, OFF_RET:]
    w_route = jnp.pad(jnp.concatenate([router_group_w, router_expert_w], axis=2),
                      ((0, 0), (0, 0), (0, LANES - N_GROUPS - N_EXPERTS)))
    half = CMP_STRIDE * HEAD_DIM
    phic = _bf16(jnp.concatenate([cmp_phi[:, :, :half], cmp_phi[:, :, half:]], axis=-1))
    cst = jnp.einsum('lcn,lcne->lce', cmp_pe.reshape(DEPTH, 2, CMP_BLOCK * HEAD_DIM), cmp_phi)[:, :, None, :]
    page_rows = PAGE_SIZE * KV_ROW
    cache_cmp = cache_cmp_kv.reshape(DEPTH, -1, page_rows, HEAD_DIM)
    cache_slc = cache_slc_kv.reshape(DEPTH, -1, page_rows, HEAD_DIM)
    wb = state_win_kv.shape[2]
    state_win = state_win_kv.reshape(DEPTH, Bs, wb * KV_ROW, HEAD_DIM)
    win_out = jnp.zeros(state_win.shape, jnp.float32)
    ret_out = jnp.zeros(state_ret.shape, jnp.float32)
    gn = ret_gn_g.reshape(DEPTH, 1, RET_WIDTH)
    outs = {k: [] for k in ("cmp_p", "slc_p", "win_p", "ret_p", "cmp_s", "slc_s")}
    for l in range(DEPTH):
        mods = (_mm(c_all, w_ada, layer=l, tm=c_rows, tn=1024, name="adaln")[:n_c] + b_ada[l]).reshape(
            n_c, N_MOD, 1, D)
        mod_p = [mods[:Bp, i] for i in range(N_MOD)]
        mod_s = [mods[Bp:, i] for i in range(N_MOD)]
        hb = _norm_mod(xp, norm_mix_g[l], mod_p[1], mod_p[0], tb=1, tt=512)
        qb, kvr, kvb, gates, rqkv, rg, zgm = _project(hb, w_in, w_gate_cols, w_tail, layer=l, tm=1024,
                                                      kv_rows=True)
        pa, pb = _compress_prompt(kvr, phic[l], n=128)
        pb = jnp.concatenate([pb[:, 1:], jnp.zeros_like(pb[:, :1])], axis=1)
        ckv = _bf16(pa + pb + jnp.repeat(cst[l], NSA_KV_HEADS, axis=0))
        o_cmp, selm = _cmp_sel(qb, ckv[:NSA_KV_HEADS], ckv[NSA_KV_HEADS:], gates, tq=256)
        o_slc = _sel_attn(qb, kvb, gates, selm, branch=1, tq=256, tk=512)
        o_win = _win_attn(qb, kvb, gates, branch=2, tq=256)
        y_ret, s_new = _ret_prompt(rqkv, rg, gn[l], chunk=math.gcd(RET_CHUNK, Tp))
        t = _merge((o_cmp, o_slc, o_win), y_ret, zgm, w_branch_nsa, w_branch_ret, layer=l, tm=1024, tn=512)
        xp = _resid_mm(t, w_out, xp, mod_p[2], layer=l, tb=1, tt=1024, tn=512)
        kv5 = kvr.reshape(3, Bp, Tp, 2, NSA_KV_HEADS, HEAD_DIM)
        outs["cmp_p"].append(kv5[0])
        outs["slc_p"].append(kv5[1])
        outs["win_p"].append(kv5[2, :, Tp - win_p:])
        outs["ret_p"].append(s_new[None])
        hb = _norm_mod(xs, norm_mix_g[l], mod_s[1], mod_s[0], tb=32, tt=Ts)
        qb, kv, _, gates, rqkv, rg, zgm = _project(hb, w_in, w_gate_cols, w_tail, layer=l, tm=Ns, kv_rows=False)
        o_nsa, win_out = _nsa_sample(qb.reshape(Bs, Ts, NSA_WIDTH), kv, gates, cache_cmp, cache_slc, state_win,
                                     win_out, page_table, phic[l], cst[l], layer=l)
        y_ret, ret_out = _ret_sample(rqkv.reshape(Bs, Ts, 3 * RET_QK), rg, gn[l], state_ret, ret_out, layer=l)
        zero = jnp.zeros_like(o_nsa)
        t = _merge((o_nsa, zero, zero), y_ret, zgm, w_branch_nsa, w_branch_ret, layer=l, tm=Ns, tn=512)
        xs = _resid_mm(t, w_out, xs, mod_s[2], layer=l, tb=Bs, tt=Ts, tn=512)
        kv6 = kv.reshape(Bs, Ts, 3, 2, NSA_KV_HEADS, HEAD_DIM)
        outs["cmp_s"].append(kv6[:, :, 0])
        outs["slc_s"].append(kv6[:, :, 1])
        hp = _norm_mod(xp, norm_ffn_g[l], mod_p[4], mod_p[3], tb=1, tt=512)
        hs = _norm_mod(xs, norm_ffn_g[l], mod_s[4], mod_s[3], tb=32, tt=Ts)
        y = _hier_moe(jnp.concatenate([hp, hs], axis=0), w_route, router_group_b[l], router_expert_b[l],
                      expert_w_gate, expert_w_up, expert_w_down, layer=l)
        xp = xp + mod_p[5] * y[:Np].reshape(Bp, Tp, D)
        xs = xs + mod_s[5] * y[Np:].reshape(Bs, Ts, D)
    y_prompt = _final_norm(xp.reshape(Np, D), norm_final_g, tm=512).reshape(Bp, Tp, D)
    y_sample = _final_norm(xs.reshape(Ns, D), norm_final_g, tm=512).reshape(Bs, Ts, D)
    return (y_prompt, y_sample, jnp.stack(outs["cmp_p"]), jnp.stack(outs["slc_p"]), jnp.stack(outs["win_p"]),
            jnp.stack(outs["ret_p"]), jnp.stack(outs["cmp_s"]), jnp.stack(outs["slc_s"]),
            win_out.reshape(state_win_kv.shape), ret_out)
```

```python
import functools
import math

import numpy as np
import jax
import jax.numpy as jnp
from jax import lax
from jax.experimental import pallas as pl
from jax.experimental.pallas import tpu as pltpu

D_MODEL = 2048
DEPTH = 2
PAGE_SIZE = 128
NSA_HEADS = 8
NSA_KV_HEADS = 2
NSA_GROUP = NSA_HEADS // NSA_KV_HEADS
HEAD_DIM = 128
NSA_WIDTH = NSA_HEADS * HEAD_DIM
CMP_STRIDE = 16
CMP_BLOCK = 2 * CMP_STRIDE
SEL_BLOCK = 64
SEG_PER_SEL = SEL_BLOCK // CMP_STRIDE
N_SELECT = 16
N_LOCAL_SEL = 2
WINDOW = 512
SCALE = HEAD_DIM ** -0.5
LOG2E = math.log2(math.e)
RET_HEADS = 4
RET_DK = 256
RET_DV = 256
RET_QK = RET_HEADS * RET_DK
RET_WIDTH = RET_HEADS * RET_DV
RET_CHUNK = 128
N_GROUPS = 4
EXPERTS_PER_GROUP = 8
N_EXPERTS = N_GROUPS * EXPERTS_PER_GROUP
TOP_K_IN_GROUP = 2
D_EXPERT = 1024
MOE_ROW_CHUNK = 256
N_MOD = 6
EPS = 1e-6
NEG_INF = -1e30
KV_COLS = 3 * 2 * NSA_KV_HEADS * HEAD_DIM
OFF_NSA_KV = NSA_WIDTH
OFF_NSA_G = OFF_NSA_KV + KV_COLS
OFF_RET = OFF_NSA_G + 3 * NSA_HEADS
OFF_MERGE = OFF_RET + 2 * RET_QK + 2 * RET_WIDTH
IN_COLS = OFF_MERGE + 2 * D_MODEL

LANES = 128
SUBLANES = 8
KV_ROW = 2 * NSA_KV_HEADS
VMEM_LIMIT_BYTES = 56 * 1024 * 1024

ALIBI = tuple(tuple(2.0 ** -(g * NSA_GROUP + r + 1) for r in range(NSA_GROUP)) for g in range(NSA_KV_HEADS))


def _cparams(*sem):
    return pltpu.CompilerParams(dimension_semantics=sem, vmem_limit_bytes=VMEM_LIMIT_BYTES)


def _bf16(x):
    return x.astype(jnp.bfloat16)


def _dot(a, b):
    return jnp.dot(a, b, preferred_element_type=jnp.float32)


def _dot_nt(a, b):
    return lax.dot_general(a, b, (((1,), (1,)), ((), ())), preferred_element_type=jnp.float32)


def _dot_tn(a, b):
    return lax.dot_general(a, b, (((0,), (0,)), ((), ())), preferred_element_type=jnp.float32)


def _norm_mod_kernel(x_ref, g_ref, sc_ref, sh_ref, o_ref):
    x = x_ref[...]
    y = x * lax.rsqrt(jnp.mean(x * x, axis=-1, keepdims=True) + EPS)
    y = y * g_ref[...]
    h = y * (1.0 + sc_ref[...]) + sh_ref[...]
    tb, tt, d = x.shape
    o_ref[...] = h.reshape(tb * tt, d).astype(o_ref.dtype)


def _norm_mod(x, g, scale, shift, *, tb, tt, out_dtype=jnp.bfloat16):
    B, T, D = x.shape
    return pl.pallas_call(
        _norm_mod_kernel,
        out_shape=jax.ShapeDtypeStruct((B * T, D), out_dtype),
        grid=(B // tb, T // tt),
        in_specs=[
            pl.BlockSpec((tb, tt, D), lambda b, t: (b, t, 0)),
            pl.BlockSpec((1, 1, D), lambda b, t: (0, 0, 0)),
            pl.BlockSpec((tb, 1, D), lambda b, t: (b, 0, 0)),
            pl.BlockSpec((tb, 1, D), lambda b, t: (b, 0, 0)),
        ],
        out_specs=pl.BlockSpec((tb * tt, D), lambda b, t: (b * (T // tt) + t, 0)),
        compiler_params=_cparams("parallel", "parallel"),
        name="norm_mod",
    )(x, g.reshape(1, 1, D), scale, shift)


def _final_norm_kernel(x_ref, g_ref, o_ref):
    x = x_ref[...]
    y = x * lax.rsqrt(jnp.mean(x * x, axis=-1, keepdims=True) + EPS)
    o_ref[...] = y * g_ref[...]


def _final_norm(x2, g, *, tm):
    M, D = x2.shape
    return pl.pallas_call(
        _final_norm_kernel,
        out_shape=jax.ShapeDtypeStruct((M, D), jnp.float32),
        grid=(M // tm,),
        in_specs=[pl.BlockSpec((tm, D), lambda i: (i, 0)), pl.BlockSpec((1, D), lambda i: (0, 0))],
        out_specs=pl.BlockSpec((tm, D), lambda i: (i, 0)),
        compiler_params=_cparams("parallel"),
        name="final_norm",
    )(x2, g.reshape(1, D))


def _mm_kernel(x_ref, w_ref, *o_refs, act):
    acc = _dot(_bf16(x_ref[...]), _bf16(w_ref[0]))
    if act == "sigmoid":
        acc = jax.nn.sigmoid(acc)
    elif act == "silu":
        acc = acc * jax.nn.sigmoid(acc)
    for o_ref in o_refs:
        o_ref[...] = acc.astype(o_ref.dtype)


def _mm(x, w3, *, layer=0, col0=0, ncols=None, tm, tn, out_dtypes=(jnp.float32,), act=None, name="mm"):
    M, K = x.shape
    ncols = w3.shape[2] - col0 if ncols is None else ncols
    cb = col0 // tn
    outs = pl.pallas_call(
        functools.partial(_mm_kernel, act=act),
        out_shape=tuple(jax.ShapeDtypeStruct((M, ncols), dt) for dt in out_dtypes),
        grid=(M // tm, ncols // tn),
        in_specs=[pl.BlockSpec((tm, K), lambda i, j: (i, 0)),
                  pl.BlockSpec((1, K, tn), lambda i, j: (layer, 0, cb + j))],
        out_specs=tuple(pl.BlockSpec((tm, tn), lambda i, j: (i, j)) for _ in out_dtypes),
        compiler_params=_cparams("parallel", "parallel"),
        name=name,
    )(x, w3)
    return outs[0] if len(outs) == 1 else outs


def _proj_kv_kernel(x_ref, w_ref, kvb_ref, rows_ref, *, tm):
    acc = _dot(x_ref[...], _bf16(w_ref[0]))
    kvb_ref[...] = _bf16(acc)
    for j in range(KV_ROW):
        rows_ref[0, pl.ds(j, tm, stride=KV_ROW), :] = acc[:, j * HEAD_DIM:(j + 1) * HEAD_DIM]


def _proj_kv(hb, w_in, *, layer, tm):
    M, K = hb.shape
    tn = KV_ROW * HEAD_DIM
    cb = OFF_NSA_KV // tn
    return pl.pallas_call(
        functools.partial(_proj_kv_kernel, tm=tm),
        out_shape=(jax.ShapeDtypeStruct((M, KV_COLS), jnp.bfloat16),
                   jax.ShapeDtypeStruct((3, M * KV_ROW, HEAD_DIM), jnp.float32)),
        grid=(M // tm, 3),
        in_specs=[pl.BlockSpec((tm, K), lambda i, j: (i, 0)),
                  pl.BlockSpec((1, K, tn), lambda i, j: (layer, 0, cb + j))],
        out_specs=(pl.BlockSpec((tm, tn), lambda i, j: (i, j)),
                   pl.BlockSpec((1, tm * KV_ROW, HEAD_DIM), lambda i, j: (j, i, 0))),
        compiler_params=_cparams("parallel", "parallel"),
        name="proj_kv_rows",
    )(hb, w_in)


def _merge_kernel(a0_ref, a1_ref, a2_ref, b_ref, g0_ref, g1_ref, wa_ref, wb_ref, o_ref):
    a = a0_ref[...] + a1_ref[...] + a2_ref[...]
    pa = _dot(_bf16(a), _bf16(wa_ref[0]))
    pb = _dot(_bf16(b_ref[...]), _bf16(wb_ref[0]))
    o_ref[...] = (jax.nn.sigmoid(g0_ref[...]) * pa + jax.nn.sigmoid(g1_ref[...]) * pb).astype(o_ref.dtype)


def _merge(o3, y_ret, zgm, w_a, w_b, *, layer, tm, tn):
    M, K = y_ret.shape
    N = w_a.shape[2]
    nj = N // tn
    row = pl.BlockSpec((tm, K), lambda i, j: (i, 0))
    return pl.pallas_call(
        _merge_kernel,
        out_shape=jax.ShapeDtypeStruct((M, N), jnp.bfloat16),
        grid=(M // tm, nj),
        in_specs=[row, row, row, row,
                  pl.BlockSpec((tm, tn), lambda i, j: (i, j)),
                  pl.BlockSpec((tm, tn), lambda i, j: (i, j + nj)),
                  pl.BlockSpec((1, K, tn), lambda i, j: (layer, 0, j)),
                  pl.BlockSpec((1, K, tn), lambda i, j: (layer, 0, j))],
        out_specs=pl.BlockSpec((tm, tn), lambda i, j: (i, j)),
        compiler_params=_cparams("parallel", "parallel"),
        name="merge_branches",
    )(*o3, y_ret, zgm, zgm, w_a, w_b)


def _resid_mm_kernel(t_ref, w_ref, x_ref, gt_ref, o_ref):
    acc = _dot(t_ref[...], _bf16(w_ref[0]))
    tb, tt, tn = x_ref.shape
    o_ref[...] = x_ref[...] + gt_ref[...] * acc.reshape(tb, tt, tn)


def _resid_mm(t, w3, x, gate, *, layer, tb, tt, tn):
    B, T, N = x.shape
    K = t.shape[1]
    nt = T // tt
    return pl.pallas_call(
        _resid_mm_kernel,
        out_shape=jax.ShapeDtypeStruct((B, T, N), jnp.float32),
        grid=(B // tb, nt, N // tn),
        in_specs=[
            pl.BlockSpec((tb * tt, K), lambda b, t_, j: (b * nt + t_, 0)),
            pl.BlockSpec((1, K, tn), lambda b, t_, j: (layer, 0, j)),
            pl.BlockSpec((tb, tt, tn), lambda b, t_, j: (b, t_, j)),
            pl.BlockSpec((tb, 1, tn), lambda b, t_, j: (b, 0, j)),
        ],
        out_specs=pl.BlockSpec((tb, tt, tn), lambda b, t_, j: (b, t_, j)),
        compiler_params=_cparams("parallel", "parallel", "parallel"),
        name="resid_out_proj",
    )(t, w3, x, gate)


def _sel_matrix_t(nb, ns, nb_pad, ns_pad):
    j = np.arange(ns_pad)[:, None]
    i = np.arange(nb_pad)[None, :]
    lo, hi = SEG_PER_SEL * j, SEG_PER_SEL * j + SEG_PER_SEL - 1
    m = ((i >= lo) & (i <= hi)).astype(np.float32) + ((i + 1 >= lo) & (i + 1 <= hi)).astype(np.float32)
    m = m * ((i < nb) & (j < ns))
    return jnp.asarray(m, jnp.bfloat16)


def _block_importance_t(imp, msel_t):
    hi = _bf16(imp)
    r1 = imp - hi.astype(jnp.float32)
    mid = _bf16(r1)
    lo = _bf16(r1 - mid.astype(jnp.float32))
    return _dot_nt(msel_t, hi) + _dot_nt(msel_t, mid) + _dot_nt(msel_t, lo)


def _pick_blocks_t(sel_t, qpos, ns):
    nsp = sel_t.shape[0]
    j = lax.broadcasted_iota(jnp.int32, (nsp, 1), 0)
    cur = qpos // SEL_BLOCK
    valid = (j <= cur) & (j < ns)
    forced = (j == 0) | (valid & (j > cur - N_LOCAL_SEL))
    score = jnp.where(forced, 1e9, jnp.where(valid, sel_t, -1e9))
    score = jnp.where(j < ns, score, -3e38)
    idx = j.astype(jnp.float32)
    chosen = jnp.zeros(sel_t.shape, jnp.float32)
    for _ in range(min(N_SELECT, ns)):
        m = jnp.max(score, axis=0, keepdims=True)
        first = jnp.min(jnp.where(score == m, idx, float(nsp)), axis=0, keepdims=True)
        hit = idx == first
        chosen = jnp.where(hit, 1.0, chosen)
        score = jnp.where(hit, -3e38, score)
    return jnp.where(valid, chosen, 0.0)


def _softmax_rows(s, mask):
    s = jnp.where(mask, s, NEG_INF)
    e = jnp.exp(s - jnp.max(s, axis=-1, keepdims=True))
    return jnp.where(mask, e * (1.0 / jnp.sum(e, axis=-1, keepdims=True)), 0.0)


def _compress_kernel(rows_ref, phic_ref, a_ref, b_ref, *, n):
    seg = KV_ROW * CMP_STRIDE
    for j in range(KV_ROW):
        x = jnp.concatenate([rows_ref[0, pl.ds(KV_ROW * l + j, n, stride=seg), :] for l in range(CMP_STRIDE)],
                            axis=1)
        y = _dot(_bf16(x), phic_ref[j // NSA_KV_HEADS])
        a_ref[j] = y[:, :HEAD_DIM]
        b_ref[j] = y[:, HEAD_DIM:]


def _compress_prompt(rows, phic, *, n):
    nseg = rows.shape[1] // (KV_ROW * CMP_STRIDE)
    out = jax.ShapeDtypeStruct((KV_ROW, nseg, HEAD_DIM), jnp.float32)
    return pl.pallas_call(
        functools.partial(_compress_kernel, n=n),
        out_shape=(out, out),
        grid=(nseg // n,),
        in_specs=[pl.BlockSpec((1, n * KV_ROW * CMP_STRIDE, HEAD_DIM), lambda i: (0, i, 0)),
                  pl.BlockSpec(phic.shape, lambda i: (0, 0, 0))],
        out_specs=(pl.BlockSpec((KV_ROW, n, HEAD_DIM), lambda i: (0, i, 0)),
                   pl.BlockSpec((KV_ROW, n, HEAD_DIM), lambda i: (0, i, 0))),
        compiler_params=_cparams("parallel"),
        name="nsa_compress",
    )(rows, phic)


def _gate_col(gate_ref, g, branch, r):
    c0 = branch * NSA_HEADS + r
    c1 = c0 + NSA_GROUP
    return jnp.where(g == 0, gate_ref[:, c0:c0 + 1], gate_ref[:, c1:c1 + 1])


def _cmp_sel_kernel(q_ref, kc_ref, vc_ref, msel_ref, gate_ref, o_ref, sel_ref, *, tq, nb, ns):
    g = pl.program_id(0)
    qi = pl.program_id(1)
    qpos = qi * tq + lax.broadcasted_iota(jnp.int32, (tq, 1), 0)
    endpos = lax.broadcasted_iota(jnp.int32, (1, nb), 1) * CMP_STRIDE + (CMP_BLOCK - 1)
    dist = qpos - endpos
    mask = dist >= 0
    distf = dist.astype(jnp.float32)
    kc = kc_ref[0]
    vc = vc_ref[0]
    imp = jnp.zeros((tq, nb), jnp.float32)
    for r in range(NSA_GROUP):
        slope = jnp.where(g == 0, ALIBI[0][r], ALIBI[1][r])
        s = _dot_nt(q_ref[:, r * HEAD_DIM:(r + 1) * HEAD_DIM], kc) * SCALE - slope * distf
        p = _softmax_rows(s, mask)
        o_ref[:, r * HEAD_DIM:(r + 1) * HEAD_DIM] = _gate_col(gate_ref, g, 0, r) * _dot(_bf16(p), vc)
        imp = imp + p
    sel_t = _block_importance_t(imp, msel_ref[...])
    qpos_row = qi * tq + lax.broadcasted_iota(jnp.int32, (1, tq), 1)
    sel_ref[0] = _pick_blocks_t(sel_t, qpos_row, ns).T.astype(sel_ref.dtype)


def _cmp_sel(qb, kc, vc, gates, *, tq):
    T = qb.shape[0]
    nb = kc.shape[1]
    ns = T // SEL_BLOCK
    gw = NSA_GROUP * HEAD_DIM
    return pl.pallas_call(
        functools.partial(_cmp_sel_kernel, tq=tq, nb=nb, ns=ns),
        out_shape=(jax.ShapeDtypeStruct((T, NSA_WIDTH), jnp.float32),
                   jax.ShapeDtypeStruct((NSA_KV_HEADS, T, ns), jnp.bfloat16)),
        grid=(NSA_KV_HEADS, T // tq),
        in_specs=[
            pl.BlockSpec((tq, gw), lambda g, i: (i, g)),
            pl.BlockSpec((1, nb, HEAD_DIM), lambda g, i: (g, 0, 0)),
            pl.BlockSpec((1, nb, HEAD_DIM), lambda g, i: (g, 0, 0)),
            pl.BlockSpec((ns, nb), lambda g, i: (0, 0)),
            pl.BlockSpec((tq, LANES), lambda g, i: (i, 0)),
        ],
        out_specs=(pl.BlockSpec((tq, gw), lambda g, i: (i, g)),
                   pl.BlockSpec((1, tq, ns), lambda g, i: (g, i, 0))),
        compiler_params=_cparams("parallel", "parallel"),
        name="nsa_cmp_select",
    )(qb, kc, vc, _sel_matrix_t(nb, ns, nb, ns), gates)


def _scores_log2(q, k, msk, slope, krel):
    z = jnp.where(msk, _dot_nt(q, k) * (SCALE * LOG2E) + (slope * LOG2E) * krel, NEG_INF)
    return [z[:, c * LANES:(c + 1) * LANES] for c in range(z.shape[1] // LANES)]


def _lane_fold(chunks, op):
    acc = chunks[0]
    for c in chunks[1:]:
        acc = op(acc, c)
    return acc


def _sel_attn_kernel(hit_ref, q_ref, k_ref, v_ref, gate_ref, sel_ref, o_ref, m_scr, l_scr, acc_scr,
                     *, tq, tk, nk, branch):
    g = pl.program_id(0)
    qi = pl.program_id(1)
    kj = pl.program_id(2)

    @pl.when(kj == 0)
    def _():
        m_scr[...] = jnp.full(m_scr.shape, NEG_INF, jnp.float32)
        l_scr[...] = jnp.zeros(l_scr.shape, jnp.float32)
        acc_scr[...] = jnp.zeros(acc_scr.shape, jnp.float32)

    touched = hit_ref[(g * pl.num_programs(1) + qi) * nk + kj] > 0

    @pl.when((kj * tk <= qi * tq + (tq - 1)) & touched)
    def _():
        k = k_ref[...]
        v = v_ref[...]
        kpos = kj * tk + lax.broadcasted_iota(jnp.int32, (1, tk), 1)
        causal = (qi * tq + lax.broadcasted_iota(jnp.int32, (tq, 1), 0)) >= kpos
        blk = lax.broadcasted_iota(jnp.int32, (sel_ref.shape[2], 1), 0)
        expand = _bf16(kpos // SEL_BLOCK == blk)
        msk = (_dot(sel_ref[0], expand) > 0.5) & causal
        krel = (kpos - qi * tq).astype(jnp.float32)
        for r in range(NSA_GROUP):
            slope = jnp.where(g == 0, ALIBI[0][r], ALIBI[1][r])
            zc = _scores_log2(q_ref[:, r * HEAD_DIM:(r + 1) * HEAD_DIM], k, msk, slope, krel)
            m_prev = m_scr[r]
            m_new = jnp.maximum(m_prev, jnp.max(_lane_fold(zc, jnp.maximum), axis=-1, keepdims=True))
            alpha = jnp.exp2(m_prev - m_new)
            pc = [jnp.exp2(z - m_new) for z in zc]
            l_scr[r] = alpha * l_scr[r] + jnp.sum(_lane_fold(pc, jnp.add), axis=-1, keepdims=True)
            acc_scr[r] = alpha * acc_scr[r] + _dot(_bf16(jnp.concatenate(pc, axis=1)), v)
            m_scr[r] = m_new

    @pl.when(kj == nk - 1)
    def _():
        for r in range(NSA_GROUP):
            o_ref[:, r * HEAD_DIM:(r + 1) * HEAD_DIM] = (
                acc_scr[r] * (_gate_col(gate_ref, g, branch, r) / l_scr[r]))


def _sel_attn(qb, kvb, gates, selm, *, branch, tq, tk):
    T = qb.shape[0]
    gw = NSA_GROUP * HEAD_DIM
    kcol = (branch * 2 + 0) * NSA_KV_HEADS
    vcol = (branch * 2 + 1) * NSA_KV_HEADS
    nk = T // tk
    ns = selm.shape[2]

    def ktile(i, j):
        return jnp.minimum(j, (i * tq + tq - 1) // tk)

    hit = jnp.max(selm.reshape(NSA_KV_HEADS, T // tq, tq, nk, tk // SEL_BLOCK).astype(jnp.float32), axis=(2, 4))
    hit = (hit > 0).astype(jnp.int32).reshape(-1)
    grid_spec = pltpu.PrefetchScalarGridSpec(
        num_scalar_prefetch=1,
        grid=(NSA_KV_HEADS, T // tq, nk),
        in_specs=[
            pl.BlockSpec((tq, gw), lambda g, i, j, h: (i, g)),
            pl.BlockSpec((tk, HEAD_DIM), lambda g, i, j, h: (ktile(i, j), kcol + g)),
            pl.BlockSpec((tk, HEAD_DIM), lambda g, i, j, h: (ktile(i, j), vcol + g)),
            pl.BlockSpec((tq, LANES), lambda g, i, j, h: (i, 0)),
            pl.BlockSpec((1, tq, ns), lambda g, i, j, h: (g, i, 0)),
        ],
        out_specs=pl.BlockSpec((tq, gw), lambda g, i, j, h: (i, g)),
        scratch_shapes=[pltpu.VMEM((NSA_GROUP, tq, LANES), jnp.float32),
                        pltpu.VMEM((NSA_GROUP, tq, LANES), jnp.float32),
                        pltpu.VMEM((NSA_GROUP, tq, HEAD_DIM), jnp.float32)],
    )
    return pl.pallas_call(
        functools.partial(_sel_attn_kernel, tq=tq, tk=tk, nk=nk, branch=branch),
        out_shape=jax.ShapeDtypeStruct((T, NSA_WIDTH), jnp.float32),
        grid_spec=grid_spec,
        compiler_params=_cparams("parallel", "parallel", "arbitrary"),
        name="nsa_sel_attn",
    )(hit, qb, kvb, kvb, gates, selm)


def _win_attn_kernel(q_ref, *refs, tq, n_prev, branch):
    k_refs = refs[:n_prev + 1]
    v_refs = refs[n_prev + 1:2 * (n_prev + 1)]
    gate_ref, o_ref = refs[2 * (n_prev + 1):]
    g = pl.program_id(0)
    qi = pl.program_id(1)
    k = jnp.concatenate([r[...] for r in k_refs], axis=0)
    v = jnp.concatenate([r[...] for r in v_refs], axis=0)
    kpos = (qi - n_prev) * tq + lax.broadcasted_iota(jnp.int32, (1, k.shape[0]), 1)
    dist = (qi * tq + lax.broadcasted_iota(jnp.int32, (tq, 1), 0)) - kpos
    msk = (dist >= 0) & (dist < WINDOW) & (kpos >= 0)
    krel = (kpos - qi * tq).astype(jnp.float32)
    for r in range(NSA_GROUP):
        slope = jnp.where(g == 0, ALIBI[0][r], ALIBI[1][r])
        zc = _scores_log2(q_ref[:, r * HEAD_DIM:(r + 1) * HEAD_DIM], k, msk, slope, krel)
        m = jnp.max(_lane_fold(zc, jnp.maximum), axis=-1, keepdims=True)
        pc = [jnp.exp2(z - m) for z in zc]
        l = jnp.sum(_lane_fold(pc, jnp.add), axis=-1, keepdims=True)
        o = _dot(_bf16(jnp.concatenate(pc, axis=1)), v)
        o_ref[:, r * HEAD_DIM:(r + 1) * HEAD_DIM] = o * (_gate_col(gate_ref, g, branch, r) / l)


def _win_attn(qb, kvb, gates, *, branch, tq):
    T = qb.shape[0]
    gw = NSA_GROUP * HEAD_DIM
    n_prev = WINDOW // tq
    kcol = (branch * 2 + 0) * NSA_KV_HEADS
    vcol = (branch * 2 + 1) * NSA_KV_HEADS

    def tile_spec(col, t):
        return pl.BlockSpec((tq, HEAD_DIM), lambda g, i: (jnp.maximum(i - n_prev + t, 0), col + g))

    return pl.pallas_call(
        functools.partial(_win_attn_kernel, tq=tq, n_prev=n_prev, branch=branch),
        out_shape=jax.ShapeDtypeStruct((T, NSA_WIDTH), jnp.float32),
        grid=(NSA_KV_HEADS, T // tq),
        in_specs=[pl.BlockSpec((tq, gw), lambda g, i: (i, g))]
        + [tile_spec(kcol, t) for t in range(n_prev + 1)] + [tile_spec(vcol, t) for t in range(n_prev + 1)]
        + [pl.BlockSpec((tq, LANES), lambda g, i: (i, 0))],
        out_specs=pl.BlockSpec((tq, gw), lambda g, i: (i, g)),
        compiler_params=_cparams("parallel", "parallel"),
        name="nsa_win_attn",
    )(qb, *([kvb] * (2 * (n_prev + 1))), gates)


def _nsa_sample_kernel(pt_ref, q_ref, kvn_ref, gate_ref, *rest, n_pages, tq, nb, ns, lk, lw):
    cmp_refs = rest[:n_pages]
    slc_refs = rest[n_pages:2 * n_pages]
    win_ref, phic_ref, cst_ref, msel_ref, perm_ref, o_ref, wout_ref, x_scr, k_scr, v_scr = rest[2 * n_pages:]
    del pt_ref
    past = n_pages * PAGE_SIZE
    wb = win_ref.shape[2] // KV_ROW
    rows = NSA_GROUP * tq
    row = lax.broadcasted_iota(jnp.int32, (rows, 1), 0)
    qpos = past + (row % tq)
    rsel = row // tq
    qf = q_ref[0].astype(jnp.float32)
    nbp = x_scr.shape[1]
    zero_tail = jnp.zeros((LANES - tq, HEAD_DIM), jnp.float32)

    def kv_new(branch, c, g):
        col = ((branch * 2 + c) * NSA_KV_HEADS + g) * HEAD_DIM
        return kvn_ref[:, col:col + HEAD_DIM]

    def gated(o, g, branch):
        for r in range(NSA_GROUP):
            c = branch * NSA_HEADS + g * NSA_GROUP + r
            col = (g * NSA_GROUP + r) * HEAD_DIM
            val = gate_ref[:, c:c + 1] * o[r * tq:(r + 1) * tq]
            if branch == 0:
                o_ref[:, col:col + HEAD_DIM] = val
            else:
                o_ref[:, col:col + HEAD_DIM] += val

    for p in range(0, n_pages, 2):
        pair = jnp.concatenate([_bf16(cmp_refs[p][0, 0]), _bf16(cmp_refs[p + 1][0, 0])], axis=1)
        y = _dot(perm_ref[...], pair)
        for half in range(2):
            rows_p = slice((p + half) * SUBLANES, (p + half + 1) * SUBLANES)
            for j in range(KV_ROW):
                for l in range(CMP_STRIDE):
                    r0 = (j * CMP_STRIDE + l) * SUBLANES
                    x_scr[j, rows_p, l * HEAD_DIM:(l + 1) * HEAD_DIM] = (
                        y[r0:r0 + SUBLANES, half * HEAD_DIM:(half + 1) * HEAD_DIM])
    base = n_pages * SUBLANES
    ckv = [[None, None], [None, None]]
    for c in range(2):
        for g in range(NSA_KV_HEADS):
            j = c * NSA_KV_HEADS + g
            x_scr[j, base:nbp, :] = jnp.zeros((nbp - base, x_scr.shape[2]), jnp.float32)
            new = kv_new(0, c, g)
            for l in range(tq):
                x_scr[j, base:base + 1, l * HEAD_DIM:(l + 1) * HEAD_DIM] = new[l:l + 1]
            y = _dot(_bf16(x_scr[j]), phic_ref[c])
            second = jnp.concatenate([y[1:, HEAD_DIM:], jnp.zeros((1, HEAD_DIM), jnp.float32)], axis=0)
            ckv[c][g] = _bf16(y[:, :HEAD_DIM] + second + cst_ref[c])

    endpos = lax.broadcasted_iota(jnp.int32, (1, nbp), 1) * CMP_STRIDE + (CMP_BLOCK - 1)
    dist_c = qpos - endpos
    mask_c = (dist_c >= 0) & (lax.broadcasted_iota(jnp.int32, (1, nbp), 1) < nb)
    kpos_s = lax.broadcasted_iota(jnp.int32, (1, lk), 1)
    dist_s = qpos - kpos_s
    blk = lax.broadcasted_iota(jnp.int32, (LANES, 1), 0)
    expand = _bf16(kpos_s // SEL_BLOCK == blk)
    kpos_w = past - wb + lax.broadcasted_iota(jnp.int32, (1, lw), 1)
    dist_w = qpos - kpos_w
    mask_w = (dist_w >= 0) & (dist_w < WINDOW)

    slopes, qgs, imps = [], [], []
    for g in range(NSA_KV_HEADS):
        slope = jnp.zeros((rows, 1), jnp.float32)
        for r in range(NSA_GROUP):
            slope = jnp.where(rsel == r, ALIBI[g][r], slope)
        qg = _bf16(jnp.concatenate(
            [qf[:, (g * NSA_GROUP + r) * HEAD_DIM:(g * NSA_GROUP + r + 1) * HEAD_DIM] for r in range(NSA_GROUP)],
            axis=0))
        slopes.append(slope)
        qgs.append(qg)
        s = _dot_nt(qg, ckv[0][g]) * SCALE - slope * dist_c.astype(jnp.float32)
        p = _softmax_rows(s, mask_c)
        gated(_dot(_bf16(p), ckv[1][g]), g, 0)
        imp = p[0:tq]
        for r in range(1, NSA_GROUP):
            imp = imp + p[r * tq:(r + 1) * tq]
        imps.append(imp)

    imp_all = jnp.concatenate(imps + [jnp.zeros((LANES - NSA_KV_HEADS * tq, nbp), jnp.float32)], axis=0)
    sel_t = _block_importance_t(imp_all, msel_ref[...])
    qpos_row = past + lax.broadcasted_iota(jnp.int32, (1, LANES), 1) % tq
    chosen = _pick_blocks_t(sel_t, qpos_row, ns).T
    keep_all = _dot(_bf16(chosen[0:2 * SUBLANES]), expand)

    for g in range(NSA_KV_HEADS):
        slope, qg = slopes[g], qgs[g]
        for p_i in range(n_pages):
            k_scr[g, p_i * PAGE_SIZE:(p_i + 1) * PAGE_SIZE, :] = _bf16(
                slc_refs[p_i][0, 0, pl.ds(g, PAGE_SIZE, stride=KV_ROW), :])
            v_scr[g, p_i * PAGE_SIZE:(p_i + 1) * PAGE_SIZE, :] = _bf16(
                slc_refs[p_i][0, 0, pl.ds(NSA_KV_HEADS + g, PAGE_SIZE, stride=KV_ROW), :])
        k_scr[g, past:lk, :] = _bf16(jnp.concatenate([kv_new(1, 0, g), zero_tail], axis=0))
        v_scr[g, past:lk, :] = _bf16(jnp.concatenate([kv_new(1, 1, g), zero_tail], axis=0))
        keep = jnp.concatenate([keep_all[g * tq:(g + 1) * tq]] * NSA_GROUP, axis=0)
        s = _dot_nt(qg, k_scr[g]) * SCALE - slope * dist_s.astype(jnp.float32)
        p = _softmax_rows(s, (keep > 0.5) & (dist_s >= 0))
        gated(_dot(_bf16(p), v_scr[g]), g, 1)
        kw = jnp.concatenate([win_ref[0, 0, pl.ds(g, wb, stride=KV_ROW), :], kv_new(2, 0, g), zero_tail], axis=0)
        vw = jnp.concatenate([win_ref[0, 0, pl.ds(NSA_KV_HEADS + g, wb, stride=KV_ROW), :], kv_new(2, 1, g),
                              zero_tail], axis=0)
        s = _dot_nt(qg, _bf16(kw)) * SCALE - slope * dist_w.astype(jnp.float32)
        p = _softmax_rows(s, mask_w)
        gated(_dot(_bf16(p), _bf16(vw)), g, 2)

    keep_rows = (wb - tq) * KV_ROW
    wout_ref[0, 0, 0:keep_rows, :] = win_ref[0, 0, tq * KV_ROW:wb * KV_ROW, :]
    for t in range(tq):
        for j in range(KV_ROW):
            col = (2 * KV_ROW + j) * HEAD_DIM
            wout_ref[0, 0, keep_rows + t * KV_ROW + j:keep_rows + t * KV_ROW + j + 1, :] = (
                kvn_ref[t:t + 1, col:col + HEAD_DIM])


def _nsa_sample(q3, kvn, gates, cache_cmp, cache_slc, state_win, win_out, page_table, phic, cst, *, layer):
    Bd, Tq, _ = q3.shape
    n_pages = page_table.shape[1]
    past = n_pages * PAGE_SIZE
    total = past + Tq
    l_pad = -(-total // SEL_BLOCK) * SEL_BLOCK
    nb = l_pad // CMP_STRIDE
    ns = l_pad // SEL_BLOCK
    nbp = -(-nb // SUBLANES) * SUBLANES
    lk = past + LANES
    wrows = state_win.shape[2]
    lw = wrows // KV_ROW + LANES
    page_rows = PAGE_SIZE * KV_ROW

    def page_spec(p):
        return pl.BlockSpec((1, 1, page_rows, HEAD_DIM), lambda b, pt: (layer, pt[b * n_pages + p], 0, 0))

    grid_spec = pltpu.PrefetchScalarGridSpec(
        num_scalar_prefetch=1,
        grid=(Bd,),
        in_specs=[pl.BlockSpec((1, Tq, NSA_WIDTH), lambda b, pt: (b, 0, 0)),
                  pl.BlockSpec((Tq, KV_COLS), lambda b, pt: (b, 0)),
                  pl.BlockSpec((Tq, LANES), lambda b, pt: (b, 0))]
        + [page_spec(p) for p in range(n_pages)] + [page_spec(p) for p in range(n_pages)]
        + [pl.BlockSpec((1, 1, wrows, HEAD_DIM), lambda b, pt: (layer, b, 0, 0)),
           pl.BlockSpec(phic.shape, lambda b, pt: (0, 0, 0)),
           pl.BlockSpec(cst.shape, lambda b, pt: (0, 0, 0)),
           pl.BlockSpec((LANES, nbp), lambda b, pt: (0, 0)),
           pl.BlockSpec((page_rows, page_rows), lambda b, pt: (0, 0)),
           pl.BlockSpec(memory_space=pl.ANY)],
        out_specs=(pl.BlockSpec((Tq, NSA_WIDTH), lambda b, pt: (b, 0)),
                   pl.BlockSpec((1, 1, wrows, HEAD_DIM), lambda b, pt: (layer, b, 0, 0))),
        scratch_shapes=[pltpu.VMEM((KV_ROW, nbp, CMP_STRIDE * HEAD_DIM), jnp.float32),
                        pltpu.VMEM((NSA_KV_HEADS, lk, HEAD_DIM), jnp.bfloat16),
                        pltpu.VMEM((NSA_KV_HEADS, lk, HEAD_DIM), jnp.bfloat16)],
    )
    n_in = 3 + 2 * n_pages + 5
    src = np.arange(page_rows)
    i_, l_, j_ = src // (KV_ROW * CMP_STRIDE), (src // KV_ROW) % CMP_STRIDE, src % KV_ROW
    perm = np.zeros((page_rows, page_rows), np.float32)
    perm[(j_ * CMP_STRIDE + l_) * (PAGE_SIZE // CMP_STRIDE) + i_, src] = 1.0

    def body(pt_ref, *refs):
        ins = refs[:n_in]
        outs_scr = refs[n_in + 1:]
        _nsa_sample_kernel(pt_ref, *ins, *outs_scr, n_pages=n_pages, tq=Tq, nb=nb, ns=ns, lk=lk, lw=lw)

    return pl.pallas_call(
        body,
        out_shape=(jax.ShapeDtypeStruct((Bd * Tq, NSA_WIDTH), jnp.float32),
                   jax.ShapeDtypeStruct(win_out.shape, win_out.dtype)),
        grid_spec=grid_spec,
        input_output_aliases={n_in + 1: 1},
        compiler_params=_cparams("arbitrary"),
        name="nsa_sample",
    )(page_table.reshape(-1), q3, kvn, gates, *([cache_cmp] * n_pages), *([cache_slc] * n_pages),
      state_win, phic, cst, _sel_matrix_t(nb, ns, nbp, LANES), jnp.asarray(perm, jnp.bfloat16), win_out)


def _decay_tables(chunk):
    log_gamma = jnp.log1p(-jnp.exp2(-5.0 - jnp.arange(RET_HEADS, dtype=jnp.float32)))
    i = jnp.arange(chunk, dtype=jnp.float32)
    diff = i[:, None] - i[None, :]
    d_in = jnp.where(diff >= 0, jnp.exp(jnp.maximum(diff, 0.0)[None] * log_gamma[:, None, None]), 0.0)
    d_q = jnp.exp((i + 1.0)[None, :] * log_gamma[:, None])[:, :, None]
    d_k = jnp.exp((chunk - 1.0 - i)[None, :] * log_gamma[:, None])[:, :, None]
    d_c = jnp.exp(chunk * log_gamma)[:, None, None]
    return d_in, d_q, d_k, d_c


def _ret_head(q, k, v, s_prev, d_in, d_q, d_k, d_c):
    kf = k * RET_DK ** -0.5
    att = _dot_nt(_bf16(q), _bf16(kf)) * d_in
    o = _dot(_bf16(att), _bf16(v)) + _dot(_bf16(q), _bf16(s_prev)) * d_q
    s_new = s_prev * d_c + _dot_tn(_bf16(kf * d_k), _bf16(v))
    return o, s_new


def _group_norm_gate(o, gn, rg):
    mu = jnp.mean(o, axis=-1, keepdims=True)
    var = jnp.mean(jnp.square(o - mu), axis=-1, keepdims=True)
    return rg * (((o - mu) * lax.rsqrt(var + EPS)) * gn)


def _ret_prompt_kernel(q_ref, k_ref, v_ref, rg_ref, gn_ref, din_ref, dq_ref, dk_ref, dc_ref,
                       y_ref, s_ref, st_scr, *, nc):
    c = pl.program_id(0)

    @pl.when(c == 0)
    def _():
        st_scr[...] = jnp.zeros(st_scr.shape, jnp.float32)

    for h in range(RET_HEADS):
        sl = slice(h * RET_DK, (h + 1) * RET_DK)
        o, s_new = _ret_head(q_ref[:, sl].astype(jnp.float32), k_ref[:, sl].astype(jnp.float32),
                             v_ref[:, sl].astype(jnp.float32), st_scr[h],
                             din_ref[h], dq_ref[h], dk_ref[h], dc_ref[h])
        st_scr[h] = s_new
        y_ref[:, sl] = _group_norm_gate(o, gn_ref[:, sl], rg_ref[:, sl]).astype(y_ref.dtype)

    @pl.when(c == nc - 1)
    def _():
        s_ref[...] = st_scr[...]


def _ret_prompt(rqkv, rg, gn, *, chunk):
    T = rqkv.shape[0]
    nc = T // chunk
    tabs = _decay_tables(chunk)
    full = lambda a: pl.BlockSpec(a.shape, lambda c: (0,) * a.ndim)
    return pl.pallas_call(
        functools.partial(_ret_prompt_kernel, nc=nc),
        out_shape=(jax.ShapeDtypeStruct((T, RET_WIDTH), jnp.bfloat16),
                   jax.ShapeDtypeStruct((RET_HEADS, RET_DK, RET_DV), jnp.float32)),
        grid=(nc,),
        in_specs=[pl.BlockSpec((chunk, RET_QK), lambda c: (c, 0)),
                  pl.BlockSpec((chunk, RET_QK), lambda c: (c, 1)),
                  pl.BlockSpec((chunk, RET_WIDTH), lambda c: (c, 2)),
                  pl.BlockSpec((chunk, RET_WIDTH), lambda c: (c, 0)),
                  full(gn)] + [full(t) for t in tabs],
        out_specs=(pl.BlockSpec((chunk, RET_WIDTH), lambda c: (c, 0)),
                   pl.BlockSpec((RET_HEADS, RET_DK, RET_DV), lambda c: (0, 0, 0))),
        scratch_shapes=[pltpu.VMEM((RET_HEADS, RET_DK, RET_DV), jnp.float32)],
        compiler_params=_cparams("arbitrary"),
        name="retention_prompt",
    )(rqkv, rqkv, rqkv, rg, gn, *tabs)


def _ret_sample_kernel(qkv_ref, rg_ref, gn_ref, din_ref, dq_ref, dk_ref, dc_ref, s0_ref, _, y_ref, s_ref, *, tq):
    x = qkv_ref[0].astype(jnp.float32)
    x = jnp.concatenate([x, jnp.zeros((2 * SUBLANES - tq, x.shape[1]), jnp.float32)], axis=0)
    for h in range(RET_HEADS):
        sl = slice(h * RET_DK, (h + 1) * RET_DK)
        o, s_new = _ret_head(x[:, h * RET_DK:(h + 1) * RET_DK],
                             x[:, RET_QK + h * RET_DK:RET_QK + (h + 1) * RET_DK],
                             x[:, 2 * RET_QK + h * RET_DV:2 * RET_QK + (h + 1) * RET_DV],
                             s0_ref[0, 0, h], din_ref[h], dq_ref[h], dk_ref[h], dc_ref[h])
        s_ref[0, 0, h] = s_new
        y_ref[:, sl] = _group_norm_gate(o[0:tq], gn_ref[:, sl], rg_ref[:, sl])


def _ret_sample(rqkv3, rg, gn, state, state_out, *, layer):
    Bd, Tq, _ = rqkv3.shape
    pad = 2 * SUBLANES
    d_in, d_q, d_k, d_c = _decay_tables(Tq)
    d_in = jnp.pad(d_in, ((0, 0), (0, pad - Tq), (0, pad - Tq)))
    d_q = jnp.pad(d_q, ((0, 0), (0, pad - Tq), (0, 0)))
    d_k = jnp.pad(d_k, ((0, 0), (0, pad - Tq), (0, 0)))
    tabs = (d_in, d_q, d_k, d_c)
    full = lambda a: pl.BlockSpec(a.shape, lambda b: (0,) * a.ndim)
    st_spec = pl.BlockSpec((1, 1, RET_HEADS, RET_DK, RET_DV), lambda b: (layer, b, 0, 0, 0))
    return pl.pallas_call(
        functools.partial(_ret_sample_kernel, tq=Tq),
        out_shape=(jax.ShapeDtypeStruct((Bd * Tq, RET_WIDTH), jnp.float32),
                   jax.ShapeDtypeStruct(state_out.shape, state_out.dtype)),
        grid=(Bd,),
        in_specs=[pl.BlockSpec((1, Tq, 3 * RET_QK), lambda b: (b, 0, 0)),
                  pl.BlockSpec((Tq, RET_WIDTH), lambda b: (b, 0)),
                  full(gn)] + [full(t) for t in tabs] + [st_spec, pl.BlockSpec(memory_space=pl.ANY)],
        out_specs=(pl.BlockSpec((Tq, RET_WIDTH), lambda b: (b, 0)), st_spec),
        input_output_aliases={8: 1},
        compiler_params=_cparams("arbitrary"),
        name="retention_sample",
    )(rqkv3, rg, gn, *tabs, state, state_out)


def _moe_kernel(tile_ref, exp_ref, lo_ref, hi_ref, first_ref, last_ref, valid_ref, src_ref, dst_ref,
                h_hbm, ws_ref, wg_ref, wu_ref, wd_ref, out_hbm,
                x32_scr, xb_scr, acc_scr, wg_scr, wu_scr, wd_scr, sem, *, tm, ch, nf):
    k = pl.program_id(0)
    f = pl.program_id(1)
    del exp_ref
    base = tile_ref[k] * tm

    def row_gather(r):
        return pltpu.make_async_copy(h_hbm.at[pl.ds(src_ref[base + r], 1), :], x32_scr.at[pl.ds(r, 1), :], sem.at[0])

    def row_scatter(r):
        return pltpu.make_async_copy(acc_scr.at[pl.ds(r, 1), :], out_hbm.at[pl.ds(dst_ref[base + r], 1), :], sem.at[1])

    def for_rows(fn, unroll):
        lax.fori_loop(0, tm, lambda r, c: (fn(r), c)[1], 0, unroll=unroll)

    @pl.when((first_ref[k] == 1) & (f == 0))
    def _():
        for_rows(lambda r: row_gather(r).start(), 8)
        for_rows(lambda r: row_gather(r).wait(), 32)
        xb_scr[...] = _bf16(x32_scr[...])
        acc_scr[...] = jnp.zeros(acc_scr.shape, jnp.float32)

    @pl.when(valid_ref[k] == 1)
    def _():
        wg_scr[...] = _bf16(wg_ref[0])
        wu_scr[...] = _bf16(wu_ref[0])
        wd_scr[...] = _bf16(wd_ref[0])
        lo = lo_ref[k]
        hi = hi_ref[k]
        for c in range(tm // ch):
            r0 = base + c * ch

            @pl.when((r0 < hi) & (r0 + ch > lo))
            def _():
                x = xb_scr[c * ch:(c + 1) * ch, :]
                a = _dot(x, wg_scr[...])
                u = _dot(x, wu_scr[...])
                rows = r0 + lax.broadcasted_iota(jnp.int32, (ch, 1), 0)
                mine = (rows >= lo) & (rows < hi)
                h = jnp.where(mine, a * jax.nn.sigmoid(a) * u * ws_ref[c * ch:(c + 1) * ch, :], 0.0)
                acc_scr[c * ch:(c + 1) * ch, :] += _dot(_bf16(h), wd_scr[...])

    @pl.when((last_ref[k] == 1) & (f == nf - 1))
    def _():
        for_rows(lambda r: row_scatter(r).start(), 8)
        for_rows(lambda r: row_scatter(r).wait(), 32)


def _moe_experts(h32, ws, meta, src, dst, w_gate, w_up, w_down, *, layer, tm, tf):
    N, D = h32.shape
    R = ws.shape[0]
    F = w_gate.shape[3]
    nf = F // tf
    n_items = meta[0].shape[0]

    def fidx(k, f, valid):
        return jnp.where(valid[k] == 1, f, nf - 1)

    grid_spec = pltpu.PrefetchScalarGridSpec(
        num_scalar_prefetch=9,
        grid=(n_items, nf),
        in_specs=[
            pl.BlockSpec(memory_space=pl.ANY),
            pl.BlockSpec((tm, 1), lambda k, f, t, e, lo, hi, fi, la, va, s, d: (t[k], 0)),
            pl.BlockSpec((None, 1, D, tf),
                         lambda k, f, t, e, lo, hi, fi, la, va, s, d: (layer, e[k], 0, fidx(k, f, va))),
            pl.BlockSpec((None, 1, D, tf),
                         lambda k, f, t, e, lo, hi, fi, la, va, s, d: (layer, e[k], 0, fidx(k, f, va))),
            pl.BlockSpec((None, 1, tf, D),
                         lambda k, f, t, e, lo, hi, fi, la, va, s, d: (layer, e[k], fidx(k, f, va), 0)),
        ],
        out_specs=pl.BlockSpec(memory_space=pl.ANY),
        scratch_shapes=[pltpu.VMEM((tm, D), jnp.float32), pltpu.VMEM((tm, D), jnp.bfloat16),
                        pltpu.VMEM((tm, D), jnp.float32),
                        pltpu.VMEM((D, tf), jnp.bfloat16), pltpu.VMEM((D, tf), jnp.bfloat16),
                        pltpu.VMEM((tf, D), jnp.bfloat16), pltpu.SemaphoreType.DMA((2,))],
    )
    return pl.pallas_call(
        functools.partial(_moe_kernel, tm=tm, ch=MOE_ROW_CHUNK, nf=nf),
        out_shape=jax.ShapeDtypeStruct((R, D), jnp.float32),
        grid_spec=grid_spec,
        compiler_params=_cparams("arbitrary", "arbitrary"),
        name="moe_experts",
    )(*meta, src, dst, h32, ws, w_gate, w_up, w_down)


def _moe_meta(sizes, *, tm, n_tiles):
    n_items = n_tiles + N_EXPERTS - 1
    ends = jnp.cumsum(sizes)
    starts = ends - sizes
    first_tile = starts // tm
    last_tile = jnp.maximum(ends - 1, 0) // tm
    cnt = jnp.where(sizes > 0, last_tile - first_tile + 1, 0)
    cum = jnp.cumsum(cnt)
    total = cum[-1]
    k = jnp.arange(n_items, dtype=jnp.int32)
    kk = jnp.minimum(k, total - 1)
    e = jnp.sum((cum[None, :] <= kk[:, None]).astype(jnp.int32), axis=1)
    tile = first_tile[e] + (kk - (cum[e] - cnt[e]))
    valid = (k < total).astype(jnp.int32)
    first = jnp.concatenate([jnp.ones((1,), jnp.int32), (tile[1:] != tile[:-1]).astype(jnp.int32)]) * valid
    last = jnp.where(k == total - 1, 1,
                     jnp.concatenate([(tile[1:] != tile[:-1]).astype(jnp.int32), jnp.zeros((1,), jnp.int32)])) * valid
    return (tile.astype(jnp.int32), e.astype(jnp.int32), starts[e].astype(jnp.int32), ends[e].astype(jnp.int32),
            first, last, valid)


def _hier_moe(h32, w_route, rg_b, re_b, w_gate, w_up, w_down, *, layer, tm=1024, tf=256):
    N, D = h32.shape
    n_route = N_GROUPS + N_EXPERTS
    logits = _mm(h32, w_route, layer=layer, tm=1024, tn=LANES, name="moe_router")
    g_logits = logits[:, :N_GROUPS] + rg_b
    e_logits = (logits[:, N_GROUPS:n_route] + re_b).reshape(N, N_GROUPS, EXPERTS_PER_GROUP)
    g_sel = jnp.argmax(g_logits, axis=-1)
    g_w = jnp.take_along_axis(jax.nn.softmax(g_logits, axis=-1), g_sel[:, None], axis=-1)
    e_in = jnp.take_along_axis(e_logits, g_sel[:, None, None], axis=1)[:, 0]
    top_v, top_i = lax.top_k(e_in, TOP_K_IN_GROUP)
    w = jax.nn.softmax(top_v, axis=-1) * g_w
    flat_e = (g_sel[:, None] * EXPERTS_PER_GROUP + top_i).reshape(-1).astype(jnp.int32)
    R = N * TOP_K_IN_GROUP
    onehot = (flat_e[:, None] == jnp.arange(N_EXPERTS, dtype=jnp.int32)[None, :]).astype(jnp.int32)
    csum = jnp.cumsum(onehot, axis=0)
    sizes = csum[-1]
    rank = jnp.sum(csum * onehot, axis=1) - 1
    starts = jnp.cumsum(sizes) - sizes
    dest = starts[flat_e] + rank
    flat = jnp.zeros((R,), jnp.int32).at[dest].set(jnp.arange(R, dtype=jnp.int32))
    ws = w.reshape(-1)[flat][:, None]
    src = flat // TOP_K_IN_GROUP
    dst = (flat % TOP_K_IN_GROUP) * N + src
    meta = _moe_meta(sizes, tm=tm, n_tiles=R // tm)
    ys = _moe_experts(h32, ws, meta, src, dst, w_gate, w_up, w_down, layer=layer, tm=tm, tf=tf)
    return ys


def _project(hb, w_in, w_gate_cols, w_tail, *, layer, tm, kv_rows):
    qb = _mm(hb, w_in, layer=layer, col0=0, ncols=OFF_NSA_KV, tm=tm, tn=512, out_dtypes=(jnp.bfloat16,),
             name="proj_q")
    if kv_rows:
        kvb, kv = _proj_kv(hb, w_in, layer=layer, tm=tm)
    else:
        kv = _mm(hb, w_in, layer=layer, col0=OFF_NSA_KV, ncols=KV_COLS, tm=tm, tn=512, name="proj_kv")
        kvb = None
    gates = _mm(hb, w_gate_cols, layer=layer, tm=tm, tn=LANES, act="sigmoid", name="proj_gate")
    rqkv = _mm(hb, w_tail, layer=layer, col0=0, ncols=3 * RET_QK, tm=tm, tn=512, out_dtypes=(jnp.bfloat16,),
               name="proj_ret")
    rg = _mm(hb, w_tail, layer=layer, col0=3 * RET_QK, ncols=RET_WIDTH, tm=tm, tn=512, act="silu",
             name="proj_ret_gate")
    zgm = _mm(hb, w_tail, layer=layer, col0=3 * RET_QK + RET_WIDTH, ncols=2 * D_MODEL, tm=tm, tn=512,
              name="proj_merge")
    return qb, kv, kvb, gates, rqkv, rg, zgm


def kernel(x_prompt, x_sample, c_prompt, c_sample, cache_cmp_kv, cache_slc_kv, state_win_kv, state_ret, page_table, norm_mix_g, norm_ffn_g, norm_final_g, w_ada, b_ada, w_in, cmp_pe, cmp_phi, ret_gn_g, w_branch_nsa, w_branch_ret, w_out, router_group_w, router_group_b, router_expert_w, router_expert_b, expert_w_gate, expert_w_up, expert_w_down):
    xp, xs = x_prompt, x_sample
    Bp, Tp, D = xp.shape
    Bs, Ts, _ = xs.shape
    Np, Ns = Bp * Tp, Bs * Ts
    win_p = min(WINDOW, Tp)
    n_c = Bp + Bs
    c_rows = -(-n_c // 16) * 16
    c_all = jnp.pad(jax.nn.silu(jnp.concatenate([c_prompt, c_sample], axis=0)), ((0, c_rows - n_c), (0, 0)))
    w_gate_cols = jnp.pad(w_in[:, :, OFF_NSA_G:OFF_RET], ((0, 0), (0, 0), (0, LANES - 3 * NSA_HEADS)))
    w_head = w_in[:, :, :OFF_NSA_G]
    w_tail = w_in[:, :, OFF_RET:]
    w_route = jnp.pad(jnp.concatenate([router_group_w, router_expert_w], axis=2),
                      ((0, 0), (0, 0), (0, LANES - N_GROUPS - N_EXPERTS)))
    half = CMP_STRIDE * HEAD_DIM
    phic = _bf16(jnp.concatenate([cmp_phi[:, :, :half], cmp_phi[:, :, half:]], axis=-1))
    cst = jnp.einsum('lcn,lcne->lce', cmp_pe.reshape(DEPTH, 2, CMP_BLOCK * HEAD_DIM), cmp_phi)[:, :, None, :]
    page_rows = PAGE_SIZE * KV_ROW
    cache_cmp = cache_cmp_kv.reshape(DEPTH, -1, page_rows, HEAD_DIM)
    cache_slc = cache_slc_kv.reshape(DEPTH, -1, page_rows, HEAD_DIM)
    wb = state_win_kv.shape[2]
    state_win = state_win_kv.reshape(DEPTH, Bs, wb * KV_ROW, HEAD_DIM)
    win_out = jnp.zeros(state_win.shape, jnp.float32)
    ret_out = jnp.zeros(state_ret.shape, jnp.float32)
    gn = ret_gn_g.reshape(DEPTH, 1, RET_WIDTH)
    outs = {k: [] for k in ("cmp_p", "slc_p", "win_p", "ret_p", "cmp_s", "slc_s")}
    for l in range(DEPTH):
        mods = (_mm(c_all, w_ada, layer=l, tm=c_rows, tn=1024, name="adaln")[:n_c] + b_ada[l]).reshape(
            n_c, N_MOD, 1, D)
        mod_p = [mods[:Bp, i] for i in range(N_MOD)]
        mod_s = [mods[Bp:, i] for i in range(N_MOD)]
        hb = _norm_mod(xp, norm_mix_g[l], mod_p[1], mod_p[0], tb=1, tt=512)
        qb, kvr, kvb, gates, rqkv, rg, zgm = _project(hb, w_head, w_gate_cols, w_tail, layer=l, tm=1024,
                                                      kv_rows=True)
        pa, pb = _compress_prompt(kvr, phic[l], n=128)
        pb = jnp.concatenate([pb[:, 1:], jnp.zeros_like(pb[:, :1])], axis=1)
        ckv = _bf16(pa + pb + jnp.repeat(cst[l], NSA_KV_HEADS, axis=0))
        o_cmp, selm = _cmp_sel(qb, ckv[:NSA_KV_HEADS], ckv[NSA_KV_HEADS:], gates, tq=256)
        o_slc = _sel_attn(qb, kvb, gates, selm, branch=1, tq=512, tk=512)
        o_win = _win_attn(qb, kvb, gates, branch=2, tq=256)
        y_ret, s_new = _ret_prompt(rqkv, rg, gn[l], chunk=math.gcd(RET_CHUNK, Tp))
        t = _merge((o_cmp, o_slc, o_win), y_ret, zgm, w_branch_nsa, w_branch_ret, layer=l, tm=1024, tn=512)
        xp = _resid_mm(t, w_out, xp, mod_p[2], layer=l, tb=1, tt=1024, tn=512)
        kv5 = kvr.reshape(3, Bp, Tp, 2, NSA_KV_HEADS, HEAD_DIM)
        outs["cmp_p"].append(kv5[0])
        outs["slc_p"].append(kv5[1])
        outs["win_p"].append(kv5[2, :, Tp - win_p:])
        outs["ret_p"].append(s_new[None])
        hb = _norm_mod(xs, norm_mix_g[l], mod_s[1], mod_s[0], tb=32, tt=Ts)
        qb, kv, _, gates, rqkv, rg, zgm = _project(hb, w_head, w_gate_cols, w_tail, layer=l, tm=Ns, kv_rows=False)
        o_nsa, win_out = _nsa_sample(qb.reshape(Bs, Ts, NSA_WIDTH), kv, gates, cache_cmp, cache_slc, state_win,
                                     win_out, page_table, phic[l], cst[l], layer=l)
        y_ret, ret_out = _ret_sample(rqkv.reshape(Bs, Ts, 3 * RET_QK), rg, gn[l], state_ret, ret_out, layer=l)
        zero = jnp.zeros_like(o_nsa)
        t = _merge((o_nsa, zero, zero), y_ret, zgm, w_branch_nsa, w_branch_ret, layer=l, tm=Ns, tn=512)
        xs = _resid_mm(t, w_out, xs, mod_s[2], layer=l, tb=Bs, tt=Ts, tn=512)
        kv6 = kv.reshape(Bs, Ts, 3, 2, NSA_KV_HEADS, HEAD_DIM)
        outs["cmp_s"].append(kv6[:, :, 0])
        outs["slc_s"].append(kv6[:, :, 1])
        hp = _norm_mod(xp, norm_ffn_g[l], mod_p[4], mod_p[3], tb=1, tt=512, out_dtype=jnp.float32)
        hs = _norm_mod(xs, norm_ffn_g[l], mod_s[4], mod_s[3], tb=32, tt=Ts, out_dtype=jnp.float32)
        ys = _hier_moe(jnp.concatenate([hp, hs], axis=0), w_route, router_group_b[l], router_expert_b[l],
                       expert_w_gate, expert_w_up, expert_w_down, layer=l)
        n_tok = Np + Ns
        xp = xp + mod_p[5] * (ys[:Np] + ys[n_tok:n_tok + Np]).reshape(Bp, Tp, D)
        xs = xs + mod_s[5] * (ys[Np:n_tok] + ys[n_tok + Np:]).reshape(Bs, Ts, D)
    y_prompt = _final_norm(xp.reshape(Np, D), norm_final_g, tm=512).reshape(Bp, Tp, D)
    y_sample = _final_norm(xs.reshape(Ns, D), norm_final_g, tm=512).reshape(Bs, Ts, D)
    return (y_prompt, y_sample, jnp.stack(outs["cmp_p"]), jnp.stack(outs["slc_p"]), jnp.stack(outs["win_p"]),
            jnp.stack(outs["ret_p"]), jnp.stack(outs["cmp_s"]), jnp.stack(outs["slc_s"]),
            win_out.reshape(state_win_kv.shape), ret_out)
```
